```python
import functools
import jax, jax.numpy as jnp
from jax import lax
import numpy as np

D_MODEL = 1024
BATCH = 4
SEQ = 8192
DEPTH = 2
DEC_BATCH = 32
DEC_SEQ = 1
PAST_LEN = 16384
PAGE_SIZE = 128

MIX_WIDTH = D_MODEL
ATTN_WIDTH = MIX_WIDTH // 2
HEAD_DIM = 64
N_HEADS = ATTN_WIDTH // HEAD_DIM
MOBA_BLOCK = 256
MOBA_TOPK = 3
MOBA_Q_CHUNK = 32
POOL_WIDTH = MIX_WIDTH - ATTN_WIDTH
POOL_WINDOWS = (2, 4, 8, 16)
N_POOL_GROUPS = len(POOL_WINDOWS)
POOL_GROUP_WIDTH = POOL_WIDTH // N_POOL_GROUPS
POOL_STATE_LEN = max(POOL_WINDOWS) - 1
N_EXPERTS = 16
N_EXPERT_GROUPS = 4
EXPERTS_PER_GROUP = N_EXPERTS // N_EXPERT_GROUPS
TOP_K = 2
D_EXPERT = D_MODEL // 2
MOE_BLOCK = 128
NORM_EPS = 1e-6
ADA_SCALE = 0.5

kernel_name = 'moba_pool_hybrid_moe_step'


def _rmsnorm(x, g):
    xf = x.astype(jnp.float32)
    y = xf * lax.rsqrt(jnp.mean(xf * xf, axis=-1, keepdims=True) + NORM_EPS)
    return y.astype(x.dtype) * g


def _ada(c, w, b):
    mod = jax.nn.silu(c) @ w + b
    return jnp.split(mod[:, None, :], 6, axis=-1)


def _alibi_slopes():
    return 2.0 ** (-8.0 * (jnp.arange(N_HEADS, dtype=jnp.float32) + 1.0) / N_HEADS)


def _moba_attend(qh, k_own, v_own, pos_own, tq, k_sel=None, v_sel=None, pos_sel=None, ok_sel=None):
    m = _alibi_slopes()[:, None, None]
    qf = qh.astype(jnp.float32) * HEAD_DIM ** -0.5
    l_own = jnp.einsum('bhqd,bhsd->bhqs', qf, k_own.astype(jnp.float32))
    l_own = l_own - m * (tq[:, None] - pos_own[None, :]).astype(jnp.float32)
    l_own = jnp.where(pos_own[None, :] <= tq[:, None], l_own, -jnp.inf)
    if k_sel is None:
        p = jax.nn.softmax(l_own, axis=-1)
        out = jnp.einsum('bhqs,bhsd->bhqd', p, v_own.astype(jnp.float32))
    else:
        l_sel = jnp.einsum('bhqd,bhqsd->bhqs', qf, k_sel.astype(jnp.float32))
        l_sel = l_sel - m * (tq[:, None] - pos_sel).astype(jnp.float32)
        if ok_sel is not None:
            l_sel = jnp.where(ok_sel, l_sel, -jnp.inf)
        n_sel = l_sel.shape[-1]
        p = jax.nn.softmax(jnp.concatenate([l_sel, l_own], axis=-1), axis=-1)
        out = (jnp.einsum('bhqs,bhqsd->bhqd', p[..., :n_sel], v_sel.astype(jnp.float32))
               + jnp.einsum('bhqs,bhsd->bhqd', p[..., n_sel:], v_own.astype(jnp.float32)))
    return out.astype(qh.dtype)


def _moba_prompt(q, k, v):
    b, t = q.shape[0], q.shape[1]
    n_kb = -(-t // MOBA_BLOCK)
    pad = n_kb * MOBA_BLOCK - t
    qh = q.transpose(0, 2, 1, 3)
    kh = jnp.pad(k.transpose(0, 2, 1, 3), ((0, 0), (0, 0), (0, pad), (0, 0)))
    vh = jnp.pad(v.transpose(0, 2, 1, 3), ((0, 0), (0, 0), (0, pad), (0, 0)))
    kb = kh.reshape(b, N_HEADS, n_kb, MOBA_BLOCK, HEAD_DIM)
    vb = vh.reshape(b, N_HEADS, n_kb, MOBA_BLOCK, HEAD_DIM)
    k_mean = kb.astype(jnp.float32).mean(axis=3)
    kk = min(MOBA_TOPK, n_kb - 1)
    b_idx = jnp.arange(b)[:, None, None, None]
    h_idx = jnp.arange(N_HEADS)[None, :, None, None]

    def chunk(t0):
        qc = lax.dynamic_slice_in_dim(qh, t0, MOBA_Q_CHUNK, axis=2)
        tq = t0 + jnp.arange(MOBA_Q_CHUNK)
        b_own = t0 // MOBA_BLOCK
        k_own = lax.dynamic_slice_in_dim(kh, b_own * MOBA_BLOCK, MOBA_BLOCK, axis=2)
        v_own = lax.dynamic_slice_in_dim(vh, b_own * MOBA_BLOCK, MOBA_BLOCK, axis=2)
        pos_own = b_own * MOBA_BLOCK + jnp.arange(MOBA_BLOCK)
        if kk == 0:
            return _moba_attend(qc, k_own, v_own, pos_own, tq)
        gate = jnp.einsum('bhqd,bhjd->bhqj', qc.astype(jnp.float32), k_mean)
        gate = jnp.where(jnp.arange(n_kb) < b_own, gate, -jnp.inf)
        _, sel = lax.top_k(gate, kk)
        ok = jnp.broadcast_to((jnp.arange(kk) < b_own)[:, None], (kk, MOBA_BLOCK)).reshape(kk * MOBA_BLOCK)
        k_sel = kb[b_idx, h_idx, sel].reshape(b, N_HEADS, MOBA_Q_CHUNK, kk * MOBA_BLOCK, HEAD_DIM)
        v_sel = vb[b_idx, h_idx, sel].reshape(b, N_HEADS, MOBA_Q_CHUNK, kk * MOBA_BLOCK, HEAD_DIM)
        pos_sel = (sel[..., None] * MOBA_BLOCK + jnp.arange(MOBA_BLOCK)).reshape(b, N_HEADS, MOBA_Q_CHUNK, kk * MOBA_BLOCK)
        return _moba_attend(qc, k_own, v_own, pos_own, tq, k_sel, v_sel, pos_sel, ok)

    outs = lax.map(chunk, jnp.arange(t // MOBA_Q_CHUNK) * MOBA_Q_CHUNK)
    return outs.transpose(1, 0, 3, 2, 4).reshape(b, t, ATTN_WIDTH)


def _moba_sample(q, k, v, cache_k, cache_v, layer, page_table):
    db, ds = q.shape[0], q.shape[1]
    ppb = MOBA_BLOCK // PAGE_SIZE
    n_fp = PAST_LEN // MOBA_BLOCK
    own_start = n_fp * MOBA_BLOCK
    n_own_pages = (PAST_LEN - own_start) // PAGE_SIZE
    qh = q.transpose(0, 2, 1, 3)
    tq = PAST_LEN + jnp.arange(ds)
    k_own, v_own, pos_own = k, v, tq
    if n_own_pages > 0:
        own_phys = page_table[:, n_fp * ppb:n_fp * ppb + n_own_pages]
        k_past = cache_k[layer, own_phys].reshape(db, n_own_pages * PAGE_SIZE, N_HEADS, HEAD_DIM)
        v_past = cache_v[layer, own_phys].reshape(db, n_own_pages * PAGE_SIZE, N_HEADS, HEAD_DIM)
        k_own = jnp.concatenate([k_past.astype(k.dtype), k], axis=1)
        v_own = jnp.concatenate([v_past.astype(v.dtype), v], axis=1)
        pos_own = jnp.concatenate([own_start + jnp.arange(n_own_pages * PAGE_SIZE), tq])
    k_own = k_own.transpose(0, 2, 1, 3)
    v_own = v_own.transpose(0, 2, 1, 3)
    kk = min(MOBA_TOPK, n_fp)
    if kk == 0:
        out = _moba_attend(qh, k_own, v_own, pos_own, tq)
    else:
        fp_phys = page_table[:, :n_fp * ppb]
        k_mean = cache_k[layer, fp_phys].astype(jnp.float32).reshape(
            db, n_fp, MOBA_BLOCK, N_HEADS, HEAD_DIM).mean(axis=2)
        gate = jnp.einsum('bhqd,bjhd->bhqj', qh.astype(jnp.float32), k_mean)
        _, sel = lax.top_k(gate, kk)
        logical = sel[..., None] * ppb + jnp.arange(ppb)
        phys = page_table[jnp.arange(db)[:, None, None, None, None], logical]
        h_idx = jnp.arange(N_HEADS)[None, :, None, None, None, None]
        rows = jnp.arange(PAGE_SIZE)
        k_sel = cache_k[layer, phys[..., None], rows, h_idx].reshape(db, N_HEADS, ds, kk * MOBA_BLOCK, HEAD_DIM)
        v_sel = cache_v[layer, phys[..., None], rows, h_idx].reshape(db, N_HEADS, ds, kk * MOBA_BLOCK, HEAD_DIM)
        pos_sel = (sel[..., None] * MOBA_BLOCK + jnp.arange(MOBA_BLOCK)).reshape(db, N_HEADS, ds, kk * MOBA_BLOCK)
        out = _moba_attend(qh, k_own, v_own, pos_own, tq, k_sel.astype(q.dtype), v_sel.astype(q.dtype), pos_sel)
    return out.transpose(0, 2, 1, 3).reshape(db, ds, ATTN_WIDTH)


def _pool_mix(u, pos, w_pool_l, pool_scale_l):
    b, l = u.shape[0], u.shape[1]
    ug = u.astype(jnp.float32).reshape(b, l, N_POOL_GROUPS, POOL_GROUP_WIDTH)
    cs = jnp.cumsum(ug, axis=1)
    means = []
    for g, w in enumerate(POOL_WINDOWS):
        csg = cs[:, :, g]
        lag = jnp.pad(csg, ((0, 0), (w, 0), (0, 0)))[:, :l]
        cnt = jnp.minimum(w, pos + 1).astype(jnp.float32)[None, :, None]
        means.append((csg - lag) / cnt)
    diff = (jnp.stack(means, axis=2) - ug).astype(u.dtype)
    out = jnp.einsum('blgc,gcd->blgd', diff, w_pool_l)
    return out.reshape(b, l, POOL_WIDTH) * pool_scale_l


def _route(h, w_router, router_bias):
    t = h.shape[0]
    scores = jax.nn.sigmoid((h @ w_router).astype(jnp.float32))
    biased = (scores + router_bias.astype(jnp.float32)).reshape(t, N_EXPERT_GROUPS, EXPERTS_PER_GROUP)
    group_score = lax.top_k(biased, 2)[0].sum(axis=-1)
    grp = jnp.argmax(group_score, axis=-1)
    in_group = biased[jnp.arange(t), grp]
    _, local = lax.top_k(in_group, TOP_K)
    idx = grp[:, None] * EXPERTS_PER_GROUP + local
    gate = jnp.take_along_axis(scores, idx, axis=1)
    return idx, gate / gate.sum(axis=-1, keepdims=True)


def _moe(h, w_router, router_bias, w1_l, w3_l, w2_l):
    b, l, d = h.shape
    t = b * l
    hf = h.reshape(t, d)
    idx, gate = _route(hf, w_router, router_bias)
    s = t * TOP_K
    flat_e = idx.reshape(s)
    order = jnp.argsort(flat_e)
    sorted_e = flat_e[order]
    counts = jnp.bincount(flat_e, length=N_EXPERTS)
    start = jnp.cumsum(counts) - counts
    padded = (counts + MOE_BLOCK - 1) // MOE_BLOCK * MOE_BLOCK
    pad_end = jnp.cumsum(padded)
    pad_start = pad_end - padded
    dest = pad_start[sorted_e] + jnp.arange(s) - start[sorted_e]
    n_blocks = -(-s // MOE_BLOCK) + N_EXPERTS
    cap = n_blocks * MOE_BLOCK
    slot_tok = jnp.full((cap,), t, jnp.int32).at[dest].set((order // TOP_K).astype(jnp.int32))
    h_pad = jnp.concatenate([hf, jnp.zeros((1, d), hf.dtype)], axis=0)
    xb = h_pad[slot_tok].reshape(n_blocks, MOE_BLOCK, d)
    block_e = jnp.minimum(jnp.searchsorted(pad_end, jnp.arange(n_blocks) * MOE_BLOCK, side='right'), N_EXPERTS - 1)

    def expert_block(args):
        xblk, e = args
        return (jax.nn.silu(xblk @ w1_l[e]) * (xblk @ w3_l[e])) @ w2_l[e]

    yb = lax.map(expert_block, (xb, block_e)).reshape(cap, d)
    y_slot = jnp.zeros((s, d), yb.dtype).at[order].set(yb[dest])
    y = jnp.einsum('tkd,tk->td', y_slot.reshape(t, TOP_K, d), gate.astype(yb.dtype))
    return y.reshape(b, l, d)


def _layer(x, c, mixer, w_ada_l, b_ada_l, g_mix_l, w_in_l, w_out_l, g_ffn_l, w_router, router_bias, w1_l, w3_l, w2_l):
    sh1, sc1, ga1, sh2, sc2, ga2 = _ada(c, w_ada_l, b_ada_l)
    h = _rmsnorm(x, g_mix_l) * (1.0 + sc1) + sh1
    b, l = h.shape[0], h.shape[1]
    q, k, v, u = jnp.split(h @ w_in_l, [ATTN_WIDTH, 2 * ATTN_WIDTH, 3 * ATTN_WIDTH], axis=-1)
    q = q.reshape(b, l, N_HEADS, HEAD_DIM)
    k = k.reshape(b, l, N_HEADS, HEAD_DIM)
    v = v.reshape(b, l, N_HEADS, HEAD_DIM)
    attn, pool, state = mixer(q, k, v, u)
    x = x + ga1 * (jnp.concatenate([attn, pool], axis=-1) @ w_out_l)
    h = _rmsnorm(x, g_ffn_l) * (1.0 + sc2) + sh2
    x = x + ga2 * _moe(h, w_router, router_bias, w1_l, w3_l, w2_l)
    return x, state


def _mixer_prompt(q, k, v, u, w_pool_l, pool_scale_l):
    t = q.shape[1]
    attn = _moba_prompt(q, k, v)
    pool = _pool_mix(u, jnp.arange(t), w_pool_l, pool_scale_l)
    return attn, pool, (k, v, u[:, t - POOL_STATE_LEN:])


def _mixer_sample(q, k, v, u, cache_k, cache_v, layer, page_table, state_pool_l, w_pool_l, pool_scale_l):
    ds = q.shape[1]
    attn = _moba_sample(q, k, v, cache_k, cache_v, layer, page_table)
    u_ext = jnp.concatenate([state_pool_l.astype(u.dtype), u], axis=1)
    pos = PAST_LEN - POOL_STATE_LEN + jnp.arange(POOL_STATE_LEN + ds)
    pool = _pool_mix(u_ext, pos, w_pool_l, pool_scale_l)[:, POOL_STATE_LEN:]
    return attn, pool, (k, v, u_ext[:, ds:])


def setup_inputs(seed: int = 0) -> dict:
    key = jax.random.key(seed)
    ks = jax.random.split(key, 24)
    f32 = jnp.float32
    n_pages = PAST_LEN // PAGE_SIZE
    n_pool_pages = (DEC_BATCH * n_pages * 5) // 4

    def nrm(k, shape, s=1.0):
        return jax.random.normal(k, shape, f32) * s

    page_table = jax.random.permutation(ks[5], n_pool_pages)[:DEC_BATCH * n_pages].reshape(DEC_BATCH, n_pages).astype(jnp.int32)
    return {
        'x_prompt': nrm(ks[0], (BATCH, SEQ, D_MODEL)),
        'x_sample': nrm(ks[1], (DEC_BATCH, DEC_SEQ, D_MODEL)),
        'cache_k': nrm(ks[2], (DEPTH, n_pool_pages, PAGE_SIZE, N_HEADS, HEAD_DIM)),
        'cache_v': nrm(ks[3], (DEPTH, n_pool_pages, PAGE_SIZE, N_HEADS, HEAD_DIM)),
        'state_pool': nrm(ks[4], (DEPTH, DEC_BATCH, POOL_STATE_LEN, POOL_WIDTH)),
        'page_table': page_table,
        'c_prompt': nrm(ks[6], (BATCH, D_MODEL)),
        'c_sample': nrm(ks[7], (DEC_BATCH, D_MODEL)),
        'w_ada': nrm(ks[8], (DEPTH, D_MODEL, 6 * D_MODEL), ADA_SCALE * D_MODEL ** -0.5),
        'b_ada': nrm(ks[9], (DEPTH, 6 * D_MODEL), 0.02),
        'g_mix': 1.0 + nrm(ks[10], (DEPTH, D_MODEL), 0.02),
        'w_in': nrm(ks[11], (DEPTH, D_MODEL, 3 * ATTN_WIDTH + POOL_WIDTH), D_MODEL ** -0.5),
        'w_pool': nrm(ks[12], (DEPTH, N_POOL_GROUPS, POOL_GROUP_WIDTH, POOL_GROUP_WIDTH), POOL_GROUP_WIDTH ** -0.5),
        'pool_scale': 1.0 + nrm(ks[13], (DEPTH, POOL_WIDTH), 0.02),
        'w_out': nrm(ks[14], (DEPTH, MIX_WIDTH, D_MODEL), MIX_WIDTH ** -0.5),
        'g_ffn': 1.0 + nrm(ks[15], (DEPTH, D_MODEL), 0.02),
        'w_router': nrm(ks[16], (D_MODEL, N_EXPERTS), D_MODEL ** -0.5),
        'router_bias': nrm(ks[17], (N_EXPERTS,), 0.01),
        'w1': nrm(ks[18], (DEPTH, N_EXPERTS, D_MODEL, D_EXPERT), D_MODEL ** -0.5),
        'w3': nrm(ks[19], (DEPTH, N_EXPERTS, D_MODEL, D_EXPERT), D_MODEL ** -0.5),
        'w2': nrm(ks[20], (DEPTH, N_EXPERTS, D_EXPERT, D_MODEL), D_EXPERT ** -0.5),
        'g_final': 1.0 + nrm(ks[21], (D_MODEL,), 0.02),
    }


def reference(x_prompt, x_sample, cache_k, cache_v, state_pool, page_table, c_prompt, c_sample,
              w_ada, b_ada, g_mix, w_in, w_pool, pool_scale, w_out, g_ffn, w_router, router_bias,
              w1, w3, w2, g_final):
    xp, xs = x_prompt, x_sample
    kp, vp, up, kq, vq, uq = [], [], [], [], [], []
    for l in range(DEPTH):
        shared = (w_ada[l], b_ada[l], g_mix[l], w_in[l], w_out[l], g_ffn[l], w_router, router_bias, w1[l], w3[l], w2[l])
        mix_p = functools.partial(_mixer_prompt, w_pool_l=w_pool[l], pool_scale_l=pool_scale[l])
        xp, (k_new, v_new, u_new) = _layer(xp, c_prompt, mix_p, *shared)
        kp.append(k_new)
        vp.append(v_new)
        up.append(u_new)
        mix_s = functools.partial(_mixer_sample, cache_k=cache_k, cache_v=cache_v, layer=l, page_table=page_table,
                                  state_pool_l=state_pool[l], w_pool_l=w_pool[l], pool_scale_l=pool_scale[l])
        xs, (k_new, v_new, u_new) = _layer(xs, c_sample, mix_s, *shared)
        kq.append(k_new)
        vq.append(v_new)
        uq.append(u_new)
    y_prompt = _rmsnorm(xp, g_final)
    y_sample = _rmsnorm(xs, g_final)
    k_prompt = jnp.stack(kp)
    v_prompt = jnp.stack(vp)
    pool_prompt = jnp.stack(up)
    k_sample = jnp.stack(kq)
    v_sample = jnp.stack(vq)
    pool_sample = jnp.stack(uq)
    return (y_prompt, y_sample, k_prompt, v_prompt, pool_prompt, k_sample, v_sample, pool_sample)
```

```python
import functools

import jax
import jax.numpy as jnp
from jax import lax
from jax.experimental import pallas as pl
from jax.experimental.pallas import tpu as pltpu

F32 = jnp.float32
BF16 = jnp.bfloat16

HEAD_DIM = 64
MOBA_BLOCK = 256
MOBA_TOPK = 3
POOL_WINDOWS = (2, 4, 8, 16)
POOL_STATE_LEN = max(POOL_WINDOWS) - 1
POOL_HALO = 16
N_EXPERT_GROUPS = 4
TOP_K = 2
NORM_EPS = 1e-6
LANES = 128
MOE_ROWS = 256
VMEM_LIMIT = 48 * 1024 * 1024

_NT = (((1,), (1,)), ((), ()))
_TN = (((0,), (0,)), ((), ()))


def _cparams(sem):
    return pltpu.CompilerParams(dimension_semantics=sem, vmem_limit_bytes=VMEM_LIMIT)


def _rms(x, g):
    return x * lax.rsqrt(jnp.mean(x * x, axis=-1, keepdims=True) + NORM_EPS) * g


def _ada_kernel(c_ref, w_ref, b_ref, o_ref):
    c = c_ref[...]
    s = c * jax.nn.sigmoid(c)
    o_ref[...] = jnp.dot(s.astype(BF16), w_ref[...].astype(BF16), preferred_element_type=F32) + b_ref[...]


def _ada(c, w_ada, b_ada):
    depth, d, n = w_ada.shape
    nb = c.shape[0]
    tn = 1536
    return pl.pallas_call(
        _ada_kernel,
        out_shape=jax.ShapeDtypeStruct((depth, nb, n), F32),
        grid=(depth, n // tn),
        in_specs=[pl.BlockSpec((nb, d), lambda l, j: (0, 0)),
                  pl.BlockSpec((None, d, tn), lambda l, j: (l, 0, j)),
                  pl.BlockSpec((None, 1, tn), lambda l, j: (l, 0, j))],
        out_specs=pl.BlockSpec((None, nb, tn), lambda l, j: (l, 0, j)),
        compiler_params=_cparams(("arbitrary", "arbitrary")),
        name="ada",
    )(c, w_ada, b_ada.reshape(depth, 1, n))


def _qkvu_kernel(x_ref, mod_ref, g_ref, w_ref, q_ref, k_ref, v_ref, kb_ref, vb_ref, u_ref, km_ref):
    a = k_ref.shape[-1]
    h = _rms(x_ref[...], g_ref[...]) * (1.0 + mod_ref[1:2, :]) + mod_ref[0:1, :]
    r = jnp.dot(h.astype(BF16), w_ref[...], preferred_element_type=F32)
    q_ref[...] = (r[:, :a] * HEAD_DIM ** -0.5).astype(BF16)
    k = r[:, a:2 * a]
    v = r[:, 2 * a:3 * a]
    k_ref[...] = k
    v_ref[...] = v
    kb_ref[...] = k.astype(BF16)
    vb_ref[...] = v.astype(BF16)
    u_ref[...] = r[:, 3 * a:]
    for s in range(km_ref.shape[0]):
        km_ref[s:s + 1, :] = jnp.mean(k[s * MOBA_BLOCK:(s + 1) * MOBA_BLOCK], axis=0, keepdims=True)


def _qkvu(x, mod, g, w_in_b, tm):
    b, l, d = x.shape
    a = w_in_b.shape[1] // 4
    nkb_t = tm // MOBA_BLOCK
    tok = lambda bi, t: (bi, t, 0)
    outs = pl.pallas_call(
        _qkvu_kernel,
        out_shape=(jax.ShapeDtypeStruct((b, l, a), BF16),
                   jax.ShapeDtypeStruct((b, l, a), F32),
                   jax.ShapeDtypeStruct((b, l, a), F32),
                   jax.ShapeDtypeStruct((b, l, a), BF16),
                   jax.ShapeDtypeStruct((b, l, a), BF16),
                   jax.ShapeDtypeStruct((b, l, a), F32),
                   jax.ShapeDtypeStruct((b, l // tm, nkb_t, a), F32)),
        grid=(b, l // tm),
        in_specs=[pl.BlockSpec((None, tm, d), tok),
                  pl.BlockSpec((None, 6, d), lambda bi, t: (bi, 0, 0)),
                  pl.BlockSpec((1, d), lambda bi, t: (0, 0)),
                  pl.BlockSpec((d, 4 * a), lambda bi, t: (0, 0))],
        out_specs=(pl.BlockSpec((None, tm, a), tok),) * 6
        + (pl.BlockSpec((None, None, nkb_t, a), lambda bi, t: (bi, t, 0, 0)),),
        compiler_params=_cparams(("arbitrary", "arbitrary")),
        name="qkvu",
    )(x, mod, g, w_in_b)
    q, k, v, kb, vb, u, km = outs
    return q, k, v, kb, vb, u, km.reshape(b, l // MOBA_BLOCK, a)


def _attn_kernel(sl_ref, q_ref, k_ref, v_ref, km_ref, o_ref, qm_ref, bias_ref, selb_ref, m_ref, l_ref, acc_ref):
    p = pl.program_id(1)
    i = pl.program_id(2)
    blk = MOBA_BLOCK
    n_kb = km_ref.shape[0]
    q2 = q_ref[...]
    lane = lax.broadcasted_iota(jnp.int32, q2.shape, 1)
    kidx = lax.broadcasted_iota(jnp.int32, (blk, blk), 0)
    qidx = lax.broadcasted_iota(jnp.int32, (blk, blk), 1)
    rel = (kidx - qidx).astype(F32)
    causal = kidx <= qidx
    km = km_ref[...].astype(BF16)
    jrow = lax.broadcasted_iota(jnp.int32, (n_kb, blk), 0)
    start = pl.multiple_of(i * blk, blk)
    kd = k_ref[pl.ds(start, blk), :]
    vd = v_ref[pl.ds(start, blk), :]

    for s in range(2):
        slope = sl_ref[2 * p + s]
        qm = jnp.where((lane >= HEAD_DIM * s) & (lane < HEAD_DIM * (s + 1)), q2, jnp.zeros_like(q2))
        qm_ref[s] = qm
        gate = lax.dot_general(km, qm, _NT, preferred_element_type=F32)
        gate = jnp.where(jrow < i, gate, -jnp.inf)
        sel = jnp.zeros(gate.shape, jnp.bool_)
        for r in range(MOBA_TOPK):
            mx = jnp.max(gate, axis=0, keepdims=True)
            idx = jnp.min(jnp.where(gate == mx, jrow, n_kb), axis=0, keepdims=True)
            pick = jrow == idx
            sel = sel | (pick & (i > r))
            gate = jnp.where(pick, -jnp.inf, gate)
        selb_ref[s] = jnp.where(sel, 0.0, jnp.inf)
        bias = slope * rel
        bias_ref[s] = bias
        st = lax.dot_general(kd, qm, _NT, preferred_element_type=F32) + bias
        st = jnp.where(causal, st, -jnp.inf)
        m = jnp.max(st, axis=0, keepdims=True)
        pt = jnp.exp(st - m)
        m_ref[s] = m
        l_ref[s] = jnp.sum(pt, axis=0, keepdims=True)
        acc_ref[s] = lax.dot_general(vd, pt.astype(BF16), _TN, preferred_element_type=F32)

    def body(j, carry):
        ks = pl.multiple_of(j * blk, blk)
        kj = k_ref[pl.ds(ks, blk), :]
        vj = v_ref[pl.ds(ks, blk), :]
        for s in range(2):
            slope = sl_ref[2 * p + s]
            cj = slope * ((j - i) * blk).astype(F32)
            st = lax.dot_general(kj, qm_ref[s], _NT, preferred_element_type=F32) + bias_ref[s]
            selrow = selb_ref[s, pl.ds(j, 1), :]
            tmax = jnp.max(st, axis=0, keepdims=True) + cj
            m_old = m_ref[s]
            m_new = jnp.where(selrow == 0.0, jnp.maximum(m_old, tmax), m_old)
            pt = jnp.exp(st - ((m_new - cj) + selrow))
            alpha = jnp.exp(m_old - m_new)
            l_ref[s] = alpha * l_ref[s] + jnp.sum(pt, axis=0, keepdims=True)
            acc_ref[s] = alpha * acc_ref[s] + lax.dot_general(vj, pt.astype(BF16), _TN, preferred_element_type=F32)
            m_ref[s] = m_new
        return carry

    lax.fori_loop(0, i, body, 0)

    o0 = acc_ref[0] / l_ref[0]
    o1 = acc_ref[1] / l_ref[1]
    ot = jnp.concatenate([o0[:HEAD_DIM], o1[HEAD_DIM:]], axis=0)
    o_ref[...] = ot.T.astype(o_ref.dtype)


def _moba_prompt(slopes, qb, kb, vb, kmean):
    b, l, a = qb.shape
    n_kb = l // MOBA_BLOCK
    blk = MOBA_BLOCK
    grid_spec = pltpu.PrefetchScalarGridSpec(
        num_scalar_prefetch=1,
        grid=(b, a // LANES, n_kb),
        in_specs=[pl.BlockSpec((None, blk, LANES), lambda bi, p, i, sl: (bi, i, p)),
                  pl.BlockSpec((None, l, LANES), lambda bi, p, i, sl: (bi, 0, p)),
                  pl.BlockSpec((None, l, LANES), lambda bi, p, i, sl: (bi, 0, p)),
                  pl.BlockSpec((None, n_kb, LANES), lambda bi, p, i, sl: (bi, 0, p))],
        out_specs=pl.BlockSpec((None, blk, LANES), lambda bi, p, i, sl: (bi, i, p)),
        scratch_shapes=[pltpu.VMEM((2, blk, LANES), BF16),
                        pltpu.VMEM((2, blk, blk), F32),
                        pltpu.VMEM((2, n_kb, blk), F32),
                        pltpu.VMEM((2, 1, blk), F32),
                        pltpu.VMEM((2, 1, blk), F32),
                        pltpu.VMEM((2, LANES, blk), F32)])
    return pl.pallas_call(
        _attn_kernel,
        out_shape=jax.ShapeDtypeStruct((b, l, a), BF16),
        grid_spec=grid_spec,
        compiler_params=_cparams(("arbitrary", "arbitrary", "arbitrary")),
        name="moba_prompt",
    )(slopes, qb, kb, vb, kmean)


def _mix_kernel(x_ref, attn_ref, u_ref, halo_ref, mod_ref, wp_ref, ps_ref, wo_ref, g_ref, wr_ref,
                xo_ref, h2_ref, lg_ref, ext_ref):
    t = pl.program_id(1)
    tm = x_ref.shape[0]
    a = attn_ref.shape[-1]
    halo = jnp.where(t == 0, 0.0, halo_ref[...])
    ext_ref[0:POOL_HALO, :] = halo
    ext_ref[POOL_HALO:POOL_HALO + tm, :] = u_ref[...]
    pos = t * tm + lax.broadcasted_iota(jnp.int32, (tm, 1), 0)
    gw = a // len(POOL_WINDOWS)
    pools = []
    for g, w in enumerate(POOL_WINDOWS):
        ln = slice(g * gw, (g + 1) * gw)
        acc = ext_ref[POOL_HALO:POOL_HALO + tm, ln]
        for s in range(1, w):
            acc = acc + ext_ref[POOL_HALO - s:POOL_HALO - s + tm, ln]
        cnt = jnp.minimum(w, pos + 1).astype(F32)
        diff = acc / cnt - u_ref[:, ln]
        pools.append(jnp.dot(diff.astype(BF16), wp_ref[g], preferred_element_type=F32))
    pool = jnp.concatenate(pools, axis=1) * ps_ref[...]
    mo = (jnp.dot(attn_ref[...], wo_ref[0:a, :], preferred_element_type=F32)
          + jnp.dot(pool.astype(BF16), wo_ref[a:, :], preferred_element_type=F32))
    xn = x_ref[...] + mod_ref[2:3, :] * mo
    xo_ref[...] = xn
    h2 = _rms(xn, g_ref[...]) * (1.0 + mod_ref[4:5, :]) + mod_ref[3:4, :]
    hb = h2.astype(BF16)
    h2_ref[...] = hb
    lg_ref[...] = jnp.dot(hb, wr_ref[...], preferred_element_type=F32)


def _mix(x, attn, u, mod, w_pool_b, pool_scale, w_out_b, g_ffn, wr_b, tm):
    b, l, d = x.shape
    a = attn.shape[-1]
    tok = lambda bi, t: (bi, t, 0)
    cst2 = lambda bi, t: (0, 0)
    hb = tm // POOL_HALO
    return pl.pallas_call(
        _mix_kernel,
        out_shape=(jax.ShapeDtypeStruct((b, l, d), F32),
                   jax.ShapeDtypeStruct((b, l, d), BF16),
                   jax.ShapeDtypeStruct((b, l, LANES), F32)),
        grid=(b, l // tm),
        in_specs=[pl.BlockSpec((None, tm, d), tok),
                  pl.BlockSpec((None, tm, a), tok),
                  pl.BlockSpec((None, tm, a), tok),
                  pl.BlockSpec((None, POOL_HALO, a), lambda bi, t: (bi, jnp.maximum(t * hb - 1, 0), 0)),
                  pl.BlockSpec((None, 6, d), lambda bi, t: (bi, 0, 0)),
                  pl.BlockSpec(w_pool_b.shape, lambda bi, t: (0, 0, 0)),
                  pl.BlockSpec((1, a), cst2),
                  pl.BlockSpec((d, d), cst2),
                  pl.BlockSpec((1, d), cst2),
                  pl.BlockSpec((d, LANES), cst2)],
        out_specs=(pl.BlockSpec((None, tm, d), tok),
                   pl.BlockSpec((None, tm, d), tok),
                   pl.BlockSpec((None, tm, LANES), tok)),
        scratch_shapes=[pltpu.VMEM((tm + POOL_HALO, a), F32)],
        compiler_params=_cparams(("arbitrary", "arbitrary")),
        name="mix",
    )(x, attn, u, u, mod, w_pool_b, pool_scale, w_out_b, g_ffn, wr_b)


def _moe_kernel(be_ref, na_ref, x_ref, w1_ref, w3_ref, w2_ref, o_ref):
    i = pl.program_id(0)

    @pl.when(i < na_ref[0])
    def _():
        x = x_ref[...]
        a = jnp.dot(x, w1_ref[...], preferred_element_type=F32)
        g = jnp.dot(x, w3_ref[...], preferred_element_type=F32)
        hmid = (a * jax.nn.sigmoid(a)) * g
        o_ref[...] = jnp.dot(hmid.astype(BF16), w2_ref[...], preferred_element_type=F32)

    @pl.when(i >= na_ref[0])
    def _():
        o_ref[...] = jnp.zeros_like(o_ref)


def _moe_blocks(block_e, n_active, xb, w1_b, w3_b, w2_b):
    cap, d = xb.shape
    de = w1_b.shape[-1]
    n_blocks = cap // MOE_ROWS
    grid_spec = pltpu.PrefetchScalarGridSpec(
        num_scalar_prefetch=2,
        grid=(n_blocks,),
        in_specs=[pl.BlockSpec((MOE_ROWS, d), lambda i, be, na: (i, 0)),
                  pl.BlockSpec((None, d, de), lambda i, be, na: (be[i], 0, 0)),
                  pl.BlockSpec((None, d, de), lambda i, be, na: (be[i], 0, 0)),
                  pl.BlockSpec((None, de, d), lambda i, be, na: (be[i], 0, 0))],
        out_specs=pl.BlockSpec((MOE_ROWS, d), lambda i, be, na: (i, 0)))
    return pl.pallas_call(
        _moe_kernel,
        out_shape=jax.ShapeDtypeStruct((cap, d), F32),
        grid_spec=grid_spec,
        compiler_params=_cparams(("arbitrary",)),
        name="moe_blocks",
    )(block_e, n_active, xb, w1_b, w3_b, w2_b)


def _route(logits, router_bias, n_experts):
    t = logits.shape[0]
    epg = n_experts // N_EXPERT_GROUPS
    scores = jax.nn.sigmoid(logits)
    biased = (scores + router_bias.astype(F32)).reshape(t, N_EXPERT_GROUPS, epg)
    group_score = lax.top_k(biased, 2)[0].sum(axis=-1)
    grp = jnp.argmax(group_score, axis=-1)
    in_group = biased[jnp.arange(t), grp]
    _, local = lax.top_k(in_group, TOP_K)
    idx = grp[:, None] * epg + local
    gate = jnp.take_along_axis(scores, idx, axis=1)
    return idx, gate / gate.sum(axis=-1, keepdims=True)


def _moe(h2b, logits, router_bias, w1_b, w3_b, w2_b):
    t, d = h2b.shape
    n_experts = w1_b.shape[0]
    idx, gate = _route(logits, router_bias, n_experts)
    s = t * TOP_K
    flat_e = idx.reshape(s)
    order = jnp.argsort(flat_e)
    sorted_e = flat_e[order]
    counts = jnp.bincount(flat_e, length=n_experts)
    start = jnp.cumsum(counts) - counts
    padded = (counts + MOE_ROWS - 1) // MOE_ROWS * MOE_ROWS
    pad_end = jnp.cumsum(padded)
    pad_start = pad_end - padded
    dest = pad_start[sorted_e] + jnp.arange(s) - start[sorted_e]
    n_blocks = -(-s // MOE_ROWS) + n_experts
    cap = n_blocks * MOE_ROWS
    slot_tok = jnp.zeros((cap,), jnp.int32).at[dest].set((order // TOP_K).astype(jnp.int32))
    xb = h2b[slot_tok]
    block_e = jnp.minimum(jnp.searchsorted(pad_end, jnp.arange(n_blocks) * MOE_ROWS, side='right'),
                          n_experts - 1).astype(jnp.int32)
    n_active = (pad_end[-1] // MOE_ROWS).astype(jnp.int32).reshape(1)
    yb = _moe_blocks(block_e, n_active, xb, w1_b, w3_b, w2_b)
    y_slot = jnp.zeros((s, d), yb.dtype).at[order].set(yb[dest])
    return jnp.einsum('tkd,tk->td', y_slot.reshape(t, TOP_K, d), gate.astype(yb.dtype))


def _final_kernel(x_ref, g_ref, o_ref):
    o_ref[...] = _rms(x_ref[...], g_ref[...])


def _final_norm(x2d, g):
    t, d = x2d.shape
    tm = min(t, 512)
    return pl.pallas_call(
        _final_kernel,
        out_shape=jax.ShapeDtypeStruct((t, d), F32),
        grid=(t // tm,),
        in_specs=[pl.BlockSpec((tm, d), lambda i: (i, 0)), pl.BlockSpec((1, d), lambda i: (0, 0))],
        out_specs=pl.BlockSpec((tm, d), lambda i: (i, 0)),
        compiler_params=_cparams(("arbitrary",)),
        name="final_norm",
    )(x2d, g)


def _alibi_slopes(n_heads):
    return 2.0 ** (-8.0 * (jnp.arange(n_heads, dtype=F32) + 1.0) / n_heads)


def _sample_attend(qh, k_own, v_own, pos_own, tq, k_sel, v_sel, pos_sel):
    n_heads = qh.shape[1]
    m = _alibi_slopes(n_heads)[:, None, None]
    qf = qh * HEAD_DIM ** -0.5
    l_own = jnp.einsum('bhqd,bhsd->bhqs', qf, k_own)
    l_own = l_own - m * (tq[:, None] - pos_own[None, :]).astype(F32)
    l_own = jnp.where(pos_own[None, :] <= tq[:, None], l_own, -jnp.inf)
    l_sel = jnp.einsum('bhqd,bhqsd->bhqs', qf, k_sel)
    l_sel = l_sel - m * (tq[:, None] - pos_sel).astype(F32)
    n_sel = l_sel.shape[-1]
    p = jax.nn.softmax(jnp.concatenate([l_sel, l_own], axis=-1), axis=-1)
    return (jnp.einsum('bhqs,bhqsd->bhqd', p[..., :n_sel], v_sel)
            + jnp.einsum('bhqs,bhsd->bhqd', p[..., n_sel:], v_own))


def _moba_sample(q, k, v, cache_k, cache_v, layer, page_table):
    db, ds, n_heads, hd = q.shape
    page = cache_k.shape[2]
    past_len = page_table.shape[1] * page
    ppb = MOBA_BLOCK // page
    n_fp = past_len // MOBA_BLOCK
    own_start = n_fp * MOBA_BLOCK
    n_own_pages = (past_len - own_start) // page
    assert n_own_pages == 0 and n_fp >= MOBA_TOPK
    qh = q.transpose(0, 2, 1, 3)
    tq = past_len + jnp.arange(ds)
    k_own = k.transpose(0, 2, 1, 3)
    v_own = v.transpose(0, 2, 1, 3)
    kk = MOBA_TOPK
    fp_phys = page_table[:, :n_fp * ppb]
    k_mean = cache_k[layer, fp_phys].reshape(db, n_fp, MOBA_BLOCK, n_heads, hd).mean(axis=2)
    gate = jnp.einsum('bhqd,bjhd->bhqj', qh, k_mean)
    _, sel = lax.top_k(gate, kk)
    logical = sel[..., None] * ppb + jnp.arange(ppb)
    phys = page_table[jnp.arange(db)[:, None, None, None, None], logical]
    h_idx = jnp.arange(n_heads)[None, :, None, None, None, None]
    rows = jnp.arange(page)
    k_sel = cache_k[layer, phys[..., None], rows, h_idx].reshape(db, n_heads, ds, kk * MOBA_BLOCK, hd)
    v_sel = cache_v[layer, phys[..., None], rows, h_idx].reshape(db, n_heads, ds, kk * MOBA_BLOCK, hd)
    pos_sel = (sel[..., None] * MOBA_BLOCK + jnp.arange(MOBA_BLOCK)).reshape(db, n_heads, ds, kk * MOBA_BLOCK)
    out = _sample_attend(qh, k_own, v_own, tq, tq, k_sel, v_sel, pos_sel)
    return out.transpose(0, 2, 1, 3).reshape(db, ds, n_heads * hd)


def _pool_mix_jax(u, pos, w_pool_l, pool_scale_l):
    b, l = u.shape[0], u.shape[1]
    ng = len(POOL_WINDOWS)
    ug = u.reshape(b, l, ng, u.shape[-1] // ng)
    cs = jnp.cumsum(ug, axis=1)
    means = []
    for g, w in enumerate(POOL_WINDOWS):
        csg = cs[:, :, g]
        lag = jnp.pad(csg, ((0, 0), (w, 0), (0, 0)))[:, :l]
        cnt = jnp.minimum(w, pos + 1).astype(F32)[None, :, None]
        means.append((csg - lag) / cnt)
    diff = jnp.stack(means, axis=2) - ug
    out = jnp.einsum('blgc,gcd->blgd', diff, w_pool_l)
    return out.reshape(b, l, -1) * pool_scale_l


def _moe_dense_jax(h, w_router, router_bias, w1_l, w3_l, w2_l):
    t = h.shape[0]
    n_experts = w1_l.shape[0]
    idx, gate = _route(h @ w_router, router_bias, n_experts)
    comb = jnp.zeros((t, n_experts), F32).at[jnp.arange(t)[:, None], idx].add(gate)
    a = jnp.einsum('td,edf->etf', h, w1_l)
    g = jnp.einsum('td,edf->etf', h, w3_l)
    y = jnp.einsum('etf,efd->etd', jax.nn.silu(a) * g, w2_l)
    return jnp.einsum('etd,te->td', y, comb)


def _sample_layer(x, mod, layer, cache_k, cache_v, page_table, state_pool_l, g_mix_l, w_in_l, w_pool_l,
                  pool_scale_l, w_out_l, g_ffn_l, w_router, router_bias, w1_l, w3_l, w2_l):
    db, ds, d = x.shape
    n_heads, hd = cache_k.shape[3], cache_k.shape[4]
    a = n_heads * hd
    sh1, sc1, ga1, sh2, sc2, ga2 = [mod[:, None, i] for i in range(6)]
    h = _rms(x, g_mix_l) * (1.0 + sc1) + sh1
    r = h @ w_in_l
    q, k, v, u = r[..., :a], r[..., a:2 * a], r[..., 2 * a:3 * a], r[..., 3 * a:]
    q = q.reshape(db, ds, n_heads, hd)
    k = k.reshape(db, ds, n_heads, hd)
    v = v.reshape(db, ds, n_heads, hd)
    attn = _moba_sample(q, k, v, cache_k, cache_v, layer, page_table)
    u_ext = jnp.concatenate([state_pool_l, u], axis=1)
    past_len = page_table.shape[1] * cache_k.shape[2]
    pos = past_len - POOL_STATE_LEN + jnp.arange(POOL_STATE_LEN + ds)
    pool = _pool_mix_jax(u_ext, pos, w_pool_l, pool_scale_l)[:, POOL_STATE_LEN:]
    x = x + ga1 * (jnp.concatenate([attn, pool], axis=-1) @ w_out_l)
    h = _rms(x, g_ffn_l) * (1.0 + sc2) + sh2
    y = _moe_dense_jax(h.reshape(db * ds, d), w_router, router_bias, w1_l, w3_l, w2_l).reshape(db, ds, d)
    x = x + ga2 * y
    return x, (k, v, u_ext[:, ds:])


def kernel(x_prompt, x_sample, cache_k, cache_v, state_pool, page_table, c_prompt, c_sample,
           w_ada, b_ada, g_mix, w_in, w_pool, pool_scale, w_out, g_ffn, w_router, router_bias,
           w1, w3, w2, g_final):
    b, l, d = x_prompt.shape
    db, ds, _ = x_sample.shape
    depth = w_ada.shape[0]
    n_heads = cache_k.shape[3]
    a = n_heads * HEAD_DIM
    n_experts = w1.shape[1]
    tm = 512
    assert l % tm == 0 and tm % MOBA_BLOCK == 0 and a % LANES == 0 and n_experts <= 16

    mod = _ada(jnp.concatenate([c_prompt, c_sample], axis=0), w_ada, b_ada)
    mod = mod.reshape(depth, b + db, 6, d)
    slopes = _alibi_slopes(n_heads)

    w_in_b = w_in.astype(BF16)
    w_out_b = w_out.astype(BF16)
    w_pool_b = w_pool.astype(BF16)
    w1_b, w3_b, w2_b = w1.astype(BF16), w3.astype(BF16), w2.astype(BF16)
    wr_b = jnp.concatenate([w_router.astype(BF16), jnp.zeros((d, LANES - n_experts), BF16)], axis=1)

    xp, xs = x_prompt, x_sample
    kp, vp, up, kq, vq, uq = [], [], [], [], [], []
    for li in range(depth):
        mod_p = mod[li, :b]
        q, k, v, kb, vb, u, kmean = _qkvu(xp, mod_p, g_mix[li][None], w_in_b[li], tm)
        attn = _moba_prompt(slopes, q, kb, vb, kmean)
        xn, h2b, lg = _mix(xp, attn, u, mod_p, w_pool_b[li], pool_scale[li][None], w_out_b[li],
                           g_ffn[li][None], wr_b, tm)
        logits = lg[..., :n_experts]
        y = _moe(h2b.reshape(b * l, d), logits.reshape(b * l, n_experts), router_bias,
                 w1_b[li], w3_b[li], w2_b[li])
        xp = xn + mod_p[:, None, 5] * y.reshape(b, l, d)
        kp.append(k.reshape(b, l, n_heads, HEAD_DIM))
        vp.append(v.reshape(b, l, n_heads, HEAD_DIM))
        up.append(u[:, l - POOL_STATE_LEN:])

        xs, (k_new, v_new, u_new) = _sample_layer(
            xs, mod[li, b:], li, cache_k, cache_v, page_table, state_pool[li], g_mix[li], w_in[li],
            w_pool[li], pool_scale[li], w_out[li], g_ffn[li], w_router, router_bias, w1[li], w3[li], w2[li])
        kq.append(k_new)
        vq.append(v_new)
        uq.append(u_new)

    y_prompt = _final_norm(xp.reshape(b * l, d), g_final[None]).reshape(b, l, d)
    y_sample = _rms(xs, g_final)
    return (y_prompt, y_sample, jnp.stack(kp), jnp.stack(vp), jnp.stack(up),
            jnp.stack(kq), jnp.stack(vq), jnp.stack(uq))
```

```python
import functools

import numpy as np
import jax
import jax.numpy as jnp
from jax import lax
from jax.experimental import pallas as pl
from jax.experimental.pallas import tpu as pltpu

F32 = jnp.float32
BF16 = jnp.bfloat16

HEAD_DIM = 64
MOBA_BLOCK = 256
MOBA_TOPK = 3
POOL_WINDOWS = (2, 4, 8, 16)
POOL_STATE_LEN = max(POOL_WINDOWS) - 1
POOL_HALO = 16
N_EXPERT_GROUPS = 4
TOP_K = 2
NORM_EPS = 1e-6
LANES = 128
HEAD_PAD = 128
ATTN_GROUP = 4
ATTN_HEADS = 4
MOE_ROWS = 256
VMEM_LIMIT = 48 * 1024 * 1024

_NT = (((1,), (1,)), ((), ()))
_TN = (((0,), (0,)), ((), ()))


def _cparams(sem):
    return pltpu.CompilerParams(dimension_semantics=sem, vmem_limit_bytes=VMEM_LIMIT)


def _rms(x, g):
    return x * lax.rsqrt(jnp.mean(x * x, axis=-1, keepdims=True) + NORM_EPS) * g


def _ada_kernel(c_ref, w_ref, b_ref, o_ref):
    c = c_ref[...]
    s = c * jax.nn.sigmoid(c)
    o_ref[...] = jnp.dot(s.astype(BF16), w_ref[...].astype(BF16), preferred_element_type=F32) + b_ref[...]


def _ada(c, w_ada, b_ada):
    depth, d, n = w_ada.shape
    nb = c.shape[0]
    tn = 1536
    return pl.pallas_call(
        _ada_kernel,
        out_shape=jax.ShapeDtypeStruct((depth, nb, n), F32),
        grid=(depth, n // tn),
        in_specs=[pl.BlockSpec((nb, d), lambda l, j: (0, 0)),
                  pl.BlockSpec((None, d, tn), lambda l, j: (l, 0, j)),
                  pl.BlockSpec((None, 1, tn), lambda l, j: (l, 0, j))],
        out_specs=pl.BlockSpec((None, nb, tn), lambda l, j: (l, 0, j)),
        compiler_params=_cparams(("arbitrary", "arbitrary")),
        name="ada",
    )(c, w_ada, b_ada.reshape(depth, 1, n))


def _qkvu_kernel(x_ref, mod_ref, g_ref, wn_ref, wa_ref, wvt_ref, srow_ref, orow_ref,
                 k_ref, v_ref, u_ref, qa_ref, ka_ref, vt_ref, km_ref):
    a = k_ref.shape[-1]
    tm = x_ref.shape[0]
    ap = qa_ref.shape[-1]
    h = _rms(x_ref[...], g_ref[...]) * (1.0 + mod_ref[1:2, :]) + mod_ref[0:1, :]
    hb = h.astype(BF16)
    rn = jnp.dot(hb, wn_ref[...], preferred_element_type=F32)
    k_ref[...] = rn[:, :a]
    v_ref[...] = rn[:, a:2 * a]
    u_ref[...] = rn[:, 2 * a:]
    ra = jnp.dot(hb, wa_ref[...], preferred_element_type=F32)
    qa_ref[...] = (ra[:, :ap] * HEAD_DIM ** -0.5 + orow_ref[...]).astype(BF16)
    kp = ra[:, ap:]
    ridx = (lax.broadcasted_iota(jnp.int32, (tm, 1), 0) % MOBA_BLOCK).astype(F32)
    ka_ref[...] = (kp + ridx * srow_ref[...]).astype(BF16)
    vt = lax.dot_general(wvt_ref[...], hb, _NT, preferred_element_type=F32)
    for s in range(km_ref.shape[0]):
        rows = slice(s * MOBA_BLOCK, (s + 1) * MOBA_BLOCK)
        km_ref[s:s + 1, :] = jnp.mean(kp[rows], axis=0, keepdims=True)
        vt_ref[s] = vt[:, rows].astype(BF16)


def _qkvu(x, mod, g, wn_b, wa_b, wvt_b, srow, orow, tm):
    b, l, d = x.shape
    a = wn_b.shape[1] // 3
    ap = wa_b.shape[1] // 2
    nkb_t = tm // MOBA_BLOCK
    tok = lambda bi, t: (bi, t, 0)
    cst = lambda bi, t: (0, 0)
    outs = pl.pallas_call(
        _qkvu_kernel,
        out_shape=(jax.ShapeDtypeStruct((b, l, a), F32),
                   jax.ShapeDtypeStruct((b, l, a), F32),
                   jax.ShapeDtypeStruct((b, l, a), F32),
                   jax.ShapeDtypeStruct((b, l, ap), BF16),
                   jax.ShapeDtypeStruct((b, l, ap), BF16),
                   jax.ShapeDtypeStruct((b, l // MOBA_BLOCK, a, MOBA_BLOCK), BF16),
                   jax.ShapeDtypeStruct((b, l // tm, nkb_t, ap), F32)),
        grid=(b, l // tm),
        in_specs=[pl.BlockSpec((None, tm, d), tok),
                  pl.BlockSpec((None, 6, d), lambda bi, t: (bi, 0, 0)),
                  pl.BlockSpec((1, d), cst),
                  pl.BlockSpec((d, 3 * a), cst),
                  pl.BlockSpec((d, 2 * ap), cst),
                  pl.BlockSpec((a, d), cst),
                  pl.BlockSpec((1, ap), cst),
                  pl.BlockSpec((1, ap), cst)],
        out_specs=(pl.BlockSpec((None, tm, a), tok),) * 3 + (pl.BlockSpec((None, tm, ap), tok),) * 2
        + (pl.BlockSpec((None, nkb_t, a, MOBA_BLOCK), lambda bi, t: (bi, t, 0, 0)),
           pl.BlockSpec((None, None, nkb_t, ap), lambda bi, t: (bi, t, 0, 0))),
        compiler_params=_cparams(("arbitrary", "arbitrary")),
        name="qkvu",
    )(x, mod, g, wn_b, wa_b, wvt_b, srow, orow)
    k, v, u, qa, ka, vt, km = outs
    return k, v, u, qa, ka, vt, km.reshape(b, l // MOBA_BLOCK, ap)


def _attn_kernel(sl_ref, q_ref, k_ref, vt_ref, km_ref, o_ref, seln_ref):
    p = pl.program_id(1)
    i = pl.program_id(2)
    blk = MOBA_BLOCK
    n_kb = km_ref.shape[0]
    nh = ATTN_HEADS
    kidx = lax.broadcasted_iota(jnp.int32, (blk, blk), 0)
    qidx = lax.broadcasted_iota(jnp.int32, (blk, blk), 1)
    causal = kidx <= qidx
    jrow = lax.broadcasted_iota(jnp.int32, (n_kb, blk), 0)
    start = pl.multiple_of(i * blk, blk)
    hls = [slice(s * HEAD_PAD, (s + 1) * HEAD_PAD) for s in range(nh)]
    hrs = [slice(s * HEAD_DIM, (s + 1) * HEAD_DIM) for s in range(nh)]
    slopes = [sl_ref[nh * p + s] for s in range(nh)]
    init = []
    for s in range(nh):
        qh = q_ref[:, hls[s]]
        gate = lax.dot_general(km_ref[:, hls[s]].astype(BF16), qh, _NT, preferred_element_type=F32)
        gate = jnp.where(jrow < i, gate, -jnp.inf)
        sel = jnp.zeros(gate.shape, jnp.bool_)
        for r in range(MOBA_TOPK):
            mx = jnp.max(gate, axis=0, keepdims=True)
            idx = jnp.min(jnp.where(gate == mx, jrow, n_kb), axis=0, keepdims=True)
            pick = jrow == idx
            sel = sel | (pick & (i > r))
            gate = jnp.where(pick, -jnp.inf, gate)
        seln_ref[s] = jnp.where(sel, 0.0, -jnp.inf)
        st = lax.dot_general(k_ref[pl.ds(start, blk), hls[s]], qh, _NT, preferred_element_type=F32)
        st = jnp.where(causal, st, -jnp.inf)
        m0 = jnp.max(st, axis=0, keepdims=True)
        pt = jnp.exp(st - m0)
        l0 = jnp.sum(pt, axis=0, keepdims=True)
        acc0 = jnp.dot(vt_ref[i, hrs[s], :], pt.astype(BF16), preferred_element_type=F32)
        init.append((m0, l0, acc0))

    def group(g, carry):
        out = []
        for s in range(nh):
            m_old, l_old, acc_old = carry[s]
            qh = q_ref[:, hls[s]]
            sts, offs = [], []
            m_new = m_old
            for u in range(ATTN_GROUP):
                j = g * ATTN_GROUP + u
                ks = pl.multiple_of(j * blk, blk)
                st_u = lax.dot_general(k_ref[pl.ds(ks, blk), hls[s]], qh, _NT, preferred_element_type=F32)
                cj = slopes[s] * ((j - i) * blk).astype(F32)
                off = cj + seln_ref[s, pl.ds(j, 1), :]
                m_new = jnp.maximum(m_new, jnp.max(st_u, axis=0, keepdims=True) + off)
                sts.append(st_u)
                offs.append(off)
            alpha = jnp.exp(m_old - m_new)
            l_new = alpha * l_old
            acc_new = alpha * acc_old
            for u in range(ATTN_GROUP):
                j = g * ATTN_GROUP + u
                pt_u = jnp.exp(sts[u] - (m_new - offs[u]))
                l_new = l_new + jnp.sum(pt_u, axis=0, keepdims=True)
                acc_new = acc_new + jnp.dot(vt_ref[j, hrs[s], :], pt_u.astype(BF16), preferred_element_type=F32)
            out.append((m_new, l_new, acc_new))
        return tuple(out)

    final = lax.fori_loop(0, (i + ATTN_GROUP - 1) // ATTN_GROUP, group, tuple(init))
    ot = jnp.concatenate([acc_f / l_f for (_, l_f, acc_f) in final], axis=0)
    o_ref[...] = ot.T.astype(o_ref.dtype)


def _moba_prompt(slopes, qa, ka, vt, kmean):
    b, l, ap = qa.shape
    n_kb = l // MOBA_BLOCK
    blk = MOBA_BLOCK
    nh = ATTN_HEADS
    n_hg = ap // (nh * HEAD_PAD)
    assert n_kb % ATTN_GROUP == 0 and ap % (nh * HEAD_PAD) == 0
    grid_spec = pltpu.PrefetchScalarGridSpec(
        num_scalar_prefetch=1,
        grid=(b, n_hg, n_kb),
        in_specs=[pl.BlockSpec((None, blk, nh * HEAD_PAD), lambda bi, p, i, sl: (bi, i, p)),
                  pl.BlockSpec((None, l, nh * HEAD_PAD), lambda bi, p, i, sl: (bi, 0, p)),
                  pl.BlockSpec((None, n_kb, nh * HEAD_DIM, blk), lambda bi, p, i, sl: (bi, 0, p, 0)),
                  pl.BlockSpec((None, n_kb, nh * HEAD_PAD), lambda bi, p, i, sl: (bi, 0, p))],
        out_specs=pl.BlockSpec((None, blk, nh * HEAD_DIM), lambda bi, p, i, sl: (bi, i, p)),
        scratch_shapes=[pltpu.VMEM((nh, n_kb, blk), F32)])
    return pl.pallas_call(
        _attn_kernel,
        out_shape=jax.ShapeDtypeStruct((b, l, n_hg * nh * HEAD_DIM), BF16),
        grid_spec=grid_spec,
        compiler_params=_cparams(("arbitrary", "arbitrary", "arbitrary")),
        name="moba_prompt",
    )(slopes, qa, ka, vt, kmean)


def _mix_kernel(x_ref, attn_ref, u_ref, halo_ref, mod_ref, wp_ref, ps_ref, wo_ref, g_ref, wr_ref,
                xo_ref, h2_ref, lg_ref, ext_ref):
    t = pl.program_id(1)
    tm = x_ref.shape[0]
    a = attn_ref.shape[-1]
    halo = jnp.where(t == 0, 0.0, halo_ref[...])
    ext_ref[0:POOL_HALO, :] = halo
    ext_ref[POOL_HALO:POOL_HALO + tm, :] = u_ref[...]
    pos = t * tm + lax.broadcasted_iota(jnp.int32, (tm, 1), 0)
    gw = a // len(POOL_WINDOWS)
    pools = []
    for g, w in enumerate(POOL_WINDOWS):
        ln = slice(g * gw, (g + 1) * gw)
        acc = ext_ref[POOL_HALO:POOL_HALO + tm, ln]
        for s in range(1, w):
            acc = acc + ext_ref[POOL_HALO - s:POOL_HALO - s + tm, ln]
        cnt = jnp.minimum(w, pos + 1).astype(F32)
        diff = acc / cnt - u_ref[:, ln]
        pools.append(jnp.dot(diff.astype(BF16), wp_ref[g], preferred_element_type=F32))
    pool = jnp.concatenate(pools, axis=1) * ps_ref[...]
    mo = (jnp.dot(attn_ref[...], wo_ref[0:a, :], preferred_element_type=F32)
          + jnp.dot(pool.astype(BF16), wo_ref[a:, :], preferred_element_type=F32))
    xn = x_ref[...] + mod_ref[2:3, :] * mo
    xo_ref[...] = xn
    h2 = _rms(xn, g_ref[...]) * (1.0 + mod_ref[4:5, :]) + mod_ref[3:4, :]
    hb = h2.astype(BF16)
    h2_ref[...] = hb
    lg_ref[...] = lax.dot_general(wr_ref[...], hb, _NT, preferred_element_type=F32)


def _mix(x, attn, u, mod, w_pool_b, pool_scale, w_out_b, g_ffn, wr_b, tm):
    b, l, d = x.shape
    a = attn.shape[-1]
    ne = wr_b.shape[0]
    tok = lambda bi, t: (bi, t, 0)
    cst2 = lambda bi, t: (0, 0)
    hb = tm // POOL_HALO
    return pl.pallas_call(
        _mix_kernel,
        out_shape=(jax.ShapeDtypeStruct((b, l, d), F32),
                   jax.ShapeDtypeStruct((b, l, d), BF16),
                   jax.ShapeDtypeStruct((b, ne, l), F32)),
        grid=(b, l // tm),
        in_specs=[pl.BlockSpec((None, tm, d), tok),
                  pl.BlockSpec((None, tm, a), tok),
                  pl.BlockSpec((None, tm, a), tok),
                  pl.BlockSpec((None, POOL_HALO, a), lambda bi, t: (bi, jnp.maximum(t * hb - 1, 0), 0)),
                  pl.BlockSpec((None, 6, d), lambda bi, t: (bi, 0, 0)),
                  pl.BlockSpec(w_pool_b.shape, lambda bi, t: (0, 0, 0)),
                  pl.BlockSpec((1, a), cst2),
                  pl.BlockSpec((d, d), cst2),
                  pl.BlockSpec((1, d), cst2),
                  pl.BlockSpec((ne, d), cst2)],
        out_specs=(pl.BlockSpec((None, tm, d), tok),
                   pl.BlockSpec((None, tm, d), tok),
                   pl.BlockSpec((None, ne, tm), lambda bi, t: (bi, 0, t))),
        scratch_shapes=[pltpu.VMEM((tm + POOL_HALO, a), F32)],
        compiler_params=_cparams(("arbitrary", "arbitrary")),
        name="mix",
    )(x, attn, u, u, mod, w_pool_b, pool_scale, w_out_b, g_ffn, wr_b)


def _moe_kernel(be_ref, na_ref, x_ref, w1_ref, w3_ref, w2_ref, o_ref):
    i = pl.program_id(0)

    @pl.when(i < na_ref[0])
    def _():
        x = x_ref[...]
        a = jnp.dot(x, w1_ref[...], preferred_element_type=F32)
        g = jnp.dot(x, w3_ref[...], preferred_element_type=F32)
        hmid = (a * jax.nn.sigmoid(a)) * g
        o_ref[...] = jnp.dot(hmid.astype(BF16), w2_ref[...], preferred_element_type=F32)

    @pl.when(i >= na_ref[0])
    def _():
        o_ref[...] = jnp.zeros_like(o_ref)


def _moe_blocks(block_e, n_active, xb, w1_b, w3_b, w2_b):
    cap, d = xb.shape
    de = w1_b.shape[-1]
    n_blocks = cap // MOE_ROWS
    grid_spec = pltpu.PrefetchScalarGridSpec(
        num_scalar_prefetch=2,
        grid=(n_blocks,),
        in_specs=[pl.BlockSpec((MOE_ROWS, d), lambda i, be, na: (i, 0)),
                  pl.BlockSpec((None, d, de), lambda i, be, na: (be[i], 0, 0)),
                  pl.BlockSpec((None, d, de), lambda i, be, na: (be[i], 0, 0)),
                  pl.BlockSpec((None, de, d), lambda i, be, na: (be[i], 0, 0))],
        out_specs=pl.BlockSpec((MOE_ROWS, d), lambda i, be, na: (i, 0)))
    return pl.pallas_call(
        _moe_kernel,
        out_shape=jax.ShapeDtypeStruct((cap, d), F32),
        grid_spec=grid_spec,
        compiler_params=_cparams(("arbitrary",)),
        name="moe_blocks",
    )(block_e, n_active, xb, w1_b, w3_b, w2_b)


def _route_rows(lg, bias):
    n_e = lg.shape[0]
    epg = n_e // N_EXPERT_GROUPS
    sc = jax.nn.sigmoid(lg)
    bi = sc + bias
    s_rows = [sc[e:e + 1, :] for e in range(n_e)]
    b_rows = [bi[e:e + 1, :] for e in range(n_e)]
    gs = []
    for g in range(N_EXPERT_GROUPS):
        v = b_rows[g * epg:(g + 1) * epg]
        best = None
        for a_ in range(epg):
            for b_ in range(a_ + 1, epg):
                pr = v[a_] + v[b_]
                best = pr if best is None else jnp.maximum(best, pr)
        gs.append(best)
    grp = jnp.zeros(gs[0].shape, jnp.int32)
    gbest = gs[0]
    for g in range(1, N_EXPERT_GROUPS):
        up = gs[g] > gbest
        gbest = jnp.where(up, gs[g], gbest)
        grp = jnp.where(up, g, grp)
    c = [b_rows[j] for j in range(epg)]
    d = [s_rows[j] for j in range(epg)]
    for g in range(1, N_EXPERT_GROUPS):
        hit = grp == g
        c = [jnp.where(hit, b_rows[g * epg + j], c[j]) for j in range(epg)]
        d = [jnp.where(hit, s_rows[g * epg + j], d[j]) for j in range(epg)]

    def first_argmax(vals):
        idx = jnp.zeros(vals[0].shape, jnp.int32)
        best = vals[0]
        for j in range(1, len(vals)):
            up = vals[j] > best
            best = jnp.where(up, vals[j], best)
            idx = jnp.where(up, j, idx)
        return idx

    i1 = first_argmax(c)
    i2 = first_argmax([jnp.where(i1 == j, -jnp.inf, c[j]) for j in range(epg)])
    g1 = d[0]
    g2 = d[0]
    for j in range(1, epg):
        g1 = jnp.where(i1 == j, d[j], g1)
        g2 = jnp.where(i2 == j, d[j], g2)
    tot = g1 + g2
    return grp * epg + i1, grp * epg + i2, g1 / tot, g2 / tot


def _route_kernel(lg_ref, bias_ref, tri_ref, io_ref, fo_ref, cnt_ref, run_ref):
    first = (pl.program_id(0) == 0) & (pl.program_id(1) == 0)

    @pl.when(first)
    def _():
        run_ref[...] = jnp.zeros_like(run_ref)

    lg = lg_ref[...]
    n_e, tr = lg.shape
    e0, e1, g0, g1 = _route_rows(lg, bias_ref[...])
    erow = lax.broadcasted_iota(jnp.int32, (n_e, tr), 0)
    oh0 = erow == e0
    oh1 = erow == e1
    both = (oh0 | oh1).astype(F32)
    before = jnp.dot(both.astype(BF16), tri_ref[...], preferred_element_type=F32) + run_ref[...]
    r0 = jnp.sum(jnp.where(oh0, before, 0.0), axis=0, keepdims=True)
    r1 = jnp.sum(jnp.where(oh1, before, 0.0), axis=0, keepdims=True)
    run_ref[...] = run_ref[...] + jnp.sum(both, axis=1, keepdims=True)
    io_ref[...] = jnp.concatenate([e0, e1, r0.astype(jnp.int32), r1.astype(jnp.int32),
                                   jnp.zeros((4, tr), jnp.int32)], axis=0)
    fo_ref[...] = jnp.concatenate([g0, g1, jnp.zeros((6, tr), F32)], axis=0)
    cnt_ref[...] = jnp.broadcast_to(run_ref[...], cnt_ref.shape)


def _route(lgt, router_bias, tr):
    b, n_e, l = lgt.shape
    tri = (jnp.arange(tr)[:, None] < jnp.arange(tr)[None, :]).astype(BF16)
    io, fo, cnt = pl.pallas_call(
        _route_kernel,
        out_shape=(jax.ShapeDtypeStruct((b, 8, l), jnp.int32),
                   jax.ShapeDtypeStruct((b, 8, l), F32),
                   jax.ShapeDtypeStruct((n_e, LANES), F32)),
        grid=(b, l // tr),
        in_specs=[pl.BlockSpec((None, n_e, tr), lambda bi, t: (bi, 0, t)),
                  pl.BlockSpec((n_e, 1), lambda bi, t: (0, 0)),
                  pl.BlockSpec((tr, tr), lambda bi, t: (0, 0))],
        out_specs=(pl.BlockSpec((None, 8, tr), lambda bi, t: (bi, 0, t)),
                   pl.BlockSpec((None, 8, tr), lambda bi, t: (bi, 0, t)),
                   pl.BlockSpec((n_e, LANES), lambda bi, t: (0, 0))),
        scratch_shapes=[pltpu.VMEM((n_e, 1), F32)],
        compiler_params=_cparams(("arbitrary", "arbitrary")),
        name="route",
    )(lgt, router_bias.astype(F32).reshape(n_e, 1), tri)
    idx = jnp.stack([io[:, 0], io[:, 1]], axis=-1).reshape(b * l, TOP_K)
    rank = jnp.stack([io[:, 2], io[:, 3]], axis=-1).reshape(b * l, TOP_K)
    gate = jnp.stack([fo[:, 0], fo[:, 1]], axis=-1).reshape(b * l, TOP_K)
    return idx, rank, gate, cnt[:, 0].astype(jnp.int32)


def _moe(h2b, lgt, router_bias, w1_b, w3_b, w2_b, tr):
    t, d = h2b.shape
    n_experts = w1_b.shape[0]
    idx, rank, gate, counts = _route(lgt, router_bias, tr)
    s = t * TOP_K
    padded = (counts + MOE_ROWS - 1) // MOE_ROWS * MOE_ROWS
    pad_end = jnp.cumsum(padded)
    pad_start = pad_end - padded
    onehot = idx[..., None] == jnp.arange(n_experts)
    dest = rank + jnp.sum(jnp.where(onehot, pad_start, 0), axis=-1)
    n_blocks = -(-s // MOE_ROWS) + n_experts
    cap = n_blocks * MOE_ROWS
    tok = jnp.broadcast_to(jnp.arange(t, dtype=jnp.int32)[:, None], (t, TOP_K))
    slot_tok = jnp.zeros((cap,), jnp.int32).at[dest.reshape(s)].set(tok.reshape(s))
    xb = h2b[slot_tok]
    block_e = jnp.minimum(jnp.searchsorted(pad_end, jnp.arange(n_blocks) * MOE_ROWS, side='right'),
                          n_experts - 1).astype(jnp.int32)
    n_active = (pad_end[-1] // MOE_ROWS).astype(jnp.int32).reshape(1)
    yb = _moe_blocks(block_e, n_active, xb, w1_b, w3_b, w2_b)
    y_slot = yb[dest.reshape(s)].reshape(t, TOP_K, d)
    return jnp.einsum('tkd,tk->td', y_slot, gate.astype(yb.dtype))


def _final_kernel(x_ref, g_ref, o_ref):
    o_ref[...] = _rms(x_ref[...], g_ref[...])


def _final_norm(x2d, g):
    t, d = x2d.shape
    tm = min(t, 512)
    return pl.pallas_call(
        _final_kernel,
        out_shape=jax.ShapeDtypeStruct((t, d), F32),
        grid=(t // tm,),
        in_specs=[pl.BlockSpec((tm, d), lambda i: (i, 0)), pl.BlockSpec((1, d), lambda i: (0, 0))],
        out_specs=pl.BlockSpec((tm, d), lambda i: (i, 0)),
        compiler_params=_cparams(("arbitrary",)),
        name="final_norm",
    )(x2d, g)


def _alibi_slopes(n_heads):
    return 2.0 ** (-8.0 * (jnp.arange(n_heads, dtype=F32) + 1.0) / n_heads)


def _sample_attend(qh, k_own, v_own, pos_own, tq, k_sel, v_sel, pos_sel):
    n_heads = qh.shape[1]
    m = _alibi_slopes(n_heads)[:, None, None]
    qf = qh * HEAD_DIM ** -0.5
    l_own = jnp.einsum('bhqd,bhsd->bhqs', qf, k_own)
    l_own = l_own - m * (tq[:, None] - pos_own[None, :]).astype(F32)
    l_own = jnp.where(pos_own[None, :] <= tq[:, None], l_own, -jnp.inf)
    l_sel = jnp.einsum('bhqd,bhqsd->bhqs', qf, k_sel)
    l_sel = l_sel - m * (tq[:, None] - pos_sel).astype(F32)
    n_sel = l_sel.shape[-1]
    p = jax.nn.softmax(jnp.concatenate([l_sel, l_own], axis=-1), axis=-1)
    return (jnp.einsum('bhqs,bhqsd->bhqd', p[..., :n_sel], v_sel)
            + jnp.einsum('bhqs,bhsd->bhqd', p[..., n_sel:], v_own))


def _moba_sample(q, k, v, cache_k, cache_v, layer, page_table):
    db, ds, n_heads, hd = q.shape
    page = cache_k.shape[2]
    past_len = page_table.shape[1] * page
    ppb = MOBA_BLOCK // page
    n_fp = past_len // MOBA_BLOCK
    own_start = n_fp * MOBA_BLOCK
    n_own_pages = (past_len - own_start) // page
    assert n_own_pages == 0 and n_fp >= MOBA_TOPK
    qh = q.transpose(0, 2, 1, 3)
    tq = past_len + jnp.arange(ds)
    k_own = k.transpose(0, 2, 1, 3)
    v_own = v.transpose(0, 2, 1, 3)
    kk = MOBA_TOPK
    fp_phys = page_table[:, :n_fp * ppb]
    k_mean = cache_k[layer, fp_phys].reshape(db, n_fp, MOBA_BLOCK, n_heads, hd).mean(axis=2)
    gate = jnp.einsum('bhqd,bjhd->bhqj', qh, k_mean)
    _, sel = lax.top_k(gate, kk)
    logical = sel[..., None] * ppb + jnp.arange(ppb)
    phys = page_table[jnp.arange(db)[:, None, None, None, None], logical]
    h_idx = jnp.arange(n_heads)[None, :, None, None, None, None]
    rows = jnp.arange(page)
    k_sel = cache_k[layer, phys[..., None], rows, h_idx].reshape(db, n_heads, ds, kk * MOBA_BLOCK, hd)
    v_sel = cache_v[layer, phys[..., None], rows, h_idx].reshape(db, n_heads, ds, kk * MOBA_BLOCK, hd)
    pos_sel = (sel[..., None] * MOBA_BLOCK + jnp.arange(MOBA_BLOCK)).reshape(db, n_heads, ds, kk * MOBA_BLOCK)
    out = _sample_attend(qh, k_own, v_own, tq, tq, k_sel, v_sel, pos_sel)
    return out.transpose(0, 2, 1, 3).reshape(db, ds, n_heads * hd)


def _pool_mix_jax(u, pos, w_pool_l, pool_scale_l):
    b, l = u.shape[0], u.shape[1]
    ng = len(POOL_WINDOWS)
    ug = u.reshape(b, l, ng, u.shape[-1] // ng)
    cs = jnp.cumsum(ug, axis=1)
    means = []
    for g, w in enumerate(POOL_WINDOWS):
        csg = cs[:, :, g]
        lag = jnp.pad(csg, ((0, 0), (w, 0), (0, 0)))[:, :l]
        cnt = jnp.minimum(w, pos + 1).astype(F32)[None, :, None]
        means.append((csg - lag) / cnt)
    diff = jnp.stack(means, axis=2) - ug
    out = jnp.einsum('blgc,gcd->blgd', diff, w_pool_l)
    return out.reshape(b, l, -1) * pool_scale_l


def _route_jnp(logits, router_bias, n_experts):
    t = logits.shape[0]
    epg = n_experts // N_EXPERT_GROUPS
    scores = jax.nn.sigmoid(logits)
    biased = (scores + router_bias.astype(F32)).reshape(t, N_EXPERT_GROUPS, epg)
    group_score = lax.top_k(biased, 2)[0].sum(axis=-1)
    grp = jnp.argmax(group_score, axis=-1)
    in_group = biased[jnp.arange(t), grp]
    _, local = lax.top_k(in_group, TOP_K)
    idx = grp[:, None] * epg + local
    gate = jnp.take_along_axis(scores, idx, axis=1)
    return idx, gate / gate.sum(axis=-1, keepdims=True)


def _moe_dense_jax(h, w_router, router_bias, w1_l, w3_l, w2_l):
    t = h.shape[0]
    n_experts = w1_l.shape[0]
    idx, gate = _route_jnp(h @ w_router, router_bias, n_experts)
    comb = jnp.zeros((t, n_experts), F32).at[jnp.arange(t)[:, None], idx].add(gate)
    a = jnp.einsum('td,edf->etf', h, w1_l)
    g = jnp.einsum('td,edf->etf', h, w3_l)
    y = jnp.einsum('etf,efd->etd', jax.nn.silu(a) * g, w2_l)
    return jnp.einsum('etd,te->td', y, comb)


def _sample_layer(x, mod, layer, cache_k, cache_v, page_table, state_pool_l, g_mix_l, w_in_l, w_pool_l,
                  pool_scale_l, w_out_l, g_ffn_l, w_router, router_bias, w1_l, w3_l, w2_l):
    db, ds, d = x.shape
    n_heads, hd = cache_k.shape[3], cache_k.shape[4]
    a = n_heads * hd
    sh1, sc1, ga1, sh2, sc2, ga2 = [mod[:, None, i] for i in range(6)]
    h = _rms(x, g_mix_l) * (1.0 + sc1) + sh1
    r = h @ w_in_l
    q, k, v, u = r[..., :a], r[..., a:2 * a], r[..., 2 * a:3 * a], r[..., 3 * a:]
    q = q.reshape(db, ds, n_heads, hd)
    k = k.reshape(db, ds, n_heads, hd)
    v = v.reshape(db, ds, n_heads, hd)
    attn = _moba_sample(q, k, v, cache_k, cache_v, layer, page_table)
    u_ext = jnp.concatenate([state_pool_l, u], axis=1)
    past_len = page_table.shape[1] * cache_k.shape[2]
    pos = past_len - POOL_STATE_LEN + jnp.arange(POOL_STATE_LEN + ds)
    pool = _pool_mix_jax(u_ext, pos, w_pool_l, pool_scale_l)[:, POOL_STATE_LEN:]
    x = x + ga1 * (jnp.concatenate([attn, pool], axis=-1) @ w_out_l)
    h = _rms(x, g_ffn_l) * (1.0 + sc2) + sh2
    y = _moe_dense_jax(h.reshape(db * ds, d), w_router, router_bias, w1_l, w3_l, w2_l).reshape(db, ds, d)
    x = x + ga2 * y
    return x, (k, v, u_ext[:, ds:])


def kernel(x_prompt, x_sample, cache_k, cache_v, state_pool, page_table, c_prompt, c_sample,
           w_ada, b_ada, g_mix, w_in, w_pool, pool_scale, w_out, g_ffn, w_router, router_bias,
           w1, w3, w2, g_final):
    b, l, d = x_prompt.shape
    db, ds, _ = x_sample.shape
    depth = w_ada.shape[0]
    n_heads = cache_k.shape[3]
    a = n_heads * HEAD_DIM
    n_experts = w1.shape[1]
    tm = 512
    assert l % tm == 0 and tm % MOBA_BLOCK == 0 and a % LANES == 0 and n_experts <= 16

    mod = _ada(jnp.concatenate([c_prompt, c_sample], axis=0), w_ada, b_ada)
    mod = mod.reshape(depth, b + db, 6, d)
    slopes = _alibi_slopes(n_heads)

    w_q, w_k, w_v = w_in[:, :, :a], w_in[:, :, a:2 * a], w_in[:, :, 2 * a:3 * a]

    def head_pad(w):
        w = w.reshape(depth, d, n_heads, HEAD_DIM)
        return jnp.pad(w, ((0, 0), (0, 0), (0, 0), (0, HEAD_PAD - HEAD_DIM))).reshape(depth, d, n_heads * HEAD_PAD)

    wn_b = w_in[:, :, a:].astype(BF16)
    wa_b = jnp.concatenate([head_pad(w_q), head_pad(w_k)], axis=-1).astype(BF16)
    wvt_b = jnp.swapaxes(w_v, 1, 2).astype(BF16)
    slopes_np = 2.0 ** (-8.0 * (np.arange(n_heads, dtype=np.float64) + 1.0) / n_heads)
    key_term = (slopes_np[:, None] * np.arange(MOBA_BLOCK)[None, :]).astype(np.float32)
    assert np.array_equal(key_term.astype(BF16).astype(np.float32), key_term), \
        "ALiBi key term must be exact in bfloat16 to ride in the padded key lanes"
    alibi_lane = np.zeros((n_heads, HEAD_PAD), np.float32)
    alibi_lane[:, HEAD_DIM] = 1.0
    orow = jnp.asarray(alibi_lane.reshape(1, -1))
    srow = jnp.asarray((alibi_lane * slopes_np[:, None].astype(np.float32)).reshape(1, -1))
    w_out_b = w_out.astype(BF16)
    w_pool_b = w_pool.astype(BF16)
    w1_b, w3_b, w2_b = w1.astype(BF16), w3.astype(BF16), w2.astype(BF16)
    wr_b = w_router.T.astype(BF16)

    xp, xs = x_prompt, x_sample
    kp, vp, up, kq, vq, uq = [], [], [], [], [], []
    for li in range(depth):
        mod_p = mod[li, :b]
        k, v, u, qa, ka, vt, kmean = _qkvu(xp, mod_p, g_mix[li][None], wn_b[li], wa_b[li], wvt_b[li], srow, orow, tm)
        attn = _moba_prompt(slopes, qa, ka, vt, kmean)
        xn, h2b, lg = _mix(xp, attn, u, mod_p, w_pool_b[li], pool_scale[li][None], w_out_b[li],
                           g_ffn[li][None], wr_b, tm)
        y = _moe(h2b.reshape(b * l, d), lg, router_bias, w1_b[li], w3_b[li], w2_b[li], tm)
        xp = xn + mod_p[:, None, 5] * y.reshape(b, l, d)
        kp.append(k.reshape(b, l, n_heads, HEAD_DIM))
        vp.append(v.reshape(b, l, n_heads, HEAD_DIM))
        up.append(u[:, l - POOL_STATE_LEN:])

        xs, (k_new, v_new, u_new) = _sample_layer(
            xs, mod[li, b:], li, cache_k, cache_v, page_table, state_pool[li], g_mix[li], w_in[li],
            w_pool[li], pool_scale[li], w_out[li], g_ffn[li], w_router, router_bias, w1[li], w3[li], w2[li])
        kq.append(k_new)
        vq.append(v_new)
        uq.append(u_new)

    y_prompt = _final_norm(xp.reshape(b * l, d), g_final[None]).reshape(b, l, d)
    y_sample = _rms(xs, g_final)
    return (y_prompt, y_sample, jnp.stack(kp), jnp.stack(vp), jnp.stack(up),
            jnp.stack(kq), jnp.stack(vq), jnp.stack(uq))
```

```python
import functools

import numpy as np
import jax
import jax.numpy as jnp
from jax import lax
from jax.experimental import pallas as pl
from jax.experimental.pallas import tpu as pltpu

F32 = jnp.float32
BF16 = jnp.bfloat16

HEAD_DIM = 64
MOBA_BLOCK = 256
MOBA_TOPK = 3
POOL_WINDOWS = (2, 4, 8, 16)
POOL_STATE_LEN = max(POOL_WINDOWS) - 1
POOL_HALO = 16
N_EXPERT_GROUPS = 4
TOP_K = 2
NORM_EPS = 1e-6
LANES = 128
HEAD_PAD = 128
ATTN_GROUP = 4
ATTN_HEADS = 4
KMEAN_PAGES = 16
MOE_TOK = 256
MOE_ROWS = 256
VMEM_LIMIT = 48 * 1024 * 1024

_NT = (((1,), (1,)), ((), ()))
_TN = (((0,), (0,)), ((), ()))


def _cparams(sem):
    return pltpu.CompilerParams(dimension_semantics=sem, vmem_limit_bytes=VMEM_LIMIT)


def _rms(x, g):
    return x * lax.rsqrt(jnp.mean(x * x, axis=-1, keepdims=True) + NORM_EPS) * g


def _ada_kernel(c_ref, w_ref, b_ref, o_ref):
    c = c_ref[...]
    s = c * jax.nn.sigmoid(c)
    o_ref[...] = jnp.dot(s.astype(BF16), w_ref[...].astype(BF16), preferred_element_type=F32) + b_ref[...]


def _ada(c, w_ada, b_ada):
    depth, d, n = w_ada.shape
    nb = c.shape[0]
    tn = 1536
    return pl.pallas_call(
        _ada_kernel,
        out_shape=jax.ShapeDtypeStruct((depth, nb, n), F32),
        grid=(depth, n // tn),
        in_specs=[pl.BlockSpec((nb, d), lambda l, j: (0, 0)),
                  pl.BlockSpec((None, d, tn), lambda l, j: (l, 0, j)),
                  pl.BlockSpec((None, 1, tn), lambda l, j: (l, 0, j))],
        out_specs=pl.BlockSpec((None, nb, tn), lambda l, j: (l, 0, j)),
        compiler_params=_cparams(("arbitrary", "arbitrary")),
        name="ada",
    )(c, w_ada, b_ada.reshape(depth, 1, n))


def _qkvu_kernel(x_ref, mod_ref, g_ref, wn_ref, wa_ref, wvt_ref, srow_ref, orow_ref,
                 k_ref, v_ref, u_ref, qa_ref, ka_ref, vt_ref, km_ref):
    a = k_ref.shape[-1]
    tm = x_ref.shape[0]
    ap = qa_ref.shape[-1]
    h = _rms(x_ref[...], g_ref[...]) * (1.0 + mod_ref[1:2, :]) + mod_ref[0:1, :]
    hb = h.astype(BF16)
    rn = jnp.dot(hb, wn_ref[...], preferred_element_type=F32)
    k_ref[...] = rn[:, :a]
    v_ref[...] = rn[:, a:2 * a]
    u_ref[...] = rn[:, 2 * a:]
    ra = jnp.dot(hb, wa_ref[...], preferred_element_type=F32)
    qa_ref[...] = (ra[:, :ap] * HEAD_DIM ** -0.5 + orow_ref[...]).astype(BF16)
    kp = ra[:, ap:]
    ridx = (lax.broadcasted_iota(jnp.int32, (tm, 1), 0) % MOBA_BLOCK).astype(F32)
    ka_ref[...] = (kp + ridx * srow_ref[...]).astype(BF16)
    vt = lax.dot_general(wvt_ref[...], hb, _NT, preferred_element_type=F32)
    for s in range(km_ref.shape[0]):
        rows = slice(s * MOBA_BLOCK, (s + 1) * MOBA_BLOCK)
        km_ref[s:s + 1, :] = jnp.mean(kp[rows], axis=0, keepdims=True)
        vt_ref[s] = vt[:, rows].astype(BF16)


def _qkvu(x, mod, g, wn_b, wa_b, wvt_b, srow, orow, tm):
    b, l, d = x.shape
    a = wn_b.shape[1] // 3
    ap = wa_b.shape[1] // 2
    nkb_t = tm // MOBA_BLOCK
    tok = lambda bi, t: (bi, t, 0)
    cst = lambda bi, t: (0, 0)
    outs = pl.pallas_call(
        _qkvu_kernel,
        out_shape=(jax.ShapeDtypeStruct((b, l, a), F32),
                   jax.ShapeDtypeStruct((b, l, a), F32),
                   jax.ShapeDtypeStruct((b, l, a), F32),
                   jax.ShapeDtypeStruct((b, l, ap), BF16),
                   jax.ShapeDtypeStruct((b, l, ap), BF16),
                   jax.ShapeDtypeStruct((b, l // MOBA_BLOCK, a, MOBA_BLOCK), BF16),
                   jax.ShapeDtypeStruct((b, l // tm, nkb_t, ap), F32)),
        grid=(b, l // tm),
        in_specs=[pl.BlockSpec((None, tm, d), tok),
                  pl.BlockSpec((None, 6, d), lambda bi, t: (bi, 0, 0)),
                  pl.BlockSpec((1, d), cst),
                  pl.BlockSpec((d, 3 * a), cst),
                  pl.BlockSpec((d, 2 * ap), cst),
                  pl.BlockSpec((a, d), cst),
                  pl.BlockSpec((1, ap), cst),
                  pl.BlockSpec((1, ap), cst)],
        out_specs=(pl.BlockSpec((None, tm, a), tok),) * 3 + (pl.BlockSpec((None, tm, ap), tok),) * 2
        + (pl.BlockSpec((None, nkb_t, a, MOBA_BLOCK), lambda bi, t: (bi, t, 0, 0)),
           pl.BlockSpec((None, None, nkb_t, ap), lambda bi, t: (bi, t, 0, 0))),
        compiler_params=_cparams(("arbitrary", "arbitrary")),
        name="qkvu",
    )(x, mod, g, wn_b, wa_b, wvt_b, srow, orow)
    k, v, u, qa, ka, vt, km = outs
    return k, v, u, qa, ka, vt, km.reshape(b, l // MOBA_BLOCK, ap)


def _attn_kernel(sl_ref, q_ref, k_ref, vt_ref, km_ref, o_ref, seln_ref):
    p = pl.program_id(1)
    i = pl.program_id(2)
    blk = MOBA_BLOCK
    n_kb = km_ref.shape[0]
    nh = ATTN_HEADS
    kidx = lax.broadcasted_iota(jnp.int32, (blk, blk), 0)
    qidx = lax.broadcasted_iota(jnp.int32, (blk, blk), 1)
    causal = kidx <= qidx
    jrow = lax.broadcasted_iota(jnp.int32, (n_kb, blk), 0)
    start = pl.multiple_of(i * blk, blk)
    hls = [slice(s * HEAD_PAD, (s + 1) * HEAD_PAD) for s in range(nh)]
    hrs = [slice(s * HEAD_DIM, (s + 1) * HEAD_DIM) for s in range(nh)]
    slopes = [sl_ref[nh * p + s] for s in range(nh)]
    init = []
    for s in range(nh):
        qh = q_ref[:, hls[s]]
        gate = lax.dot_general(km_ref[:, hls[s]].astype(BF16), qh, _NT, preferred_element_type=F32)
        gate = jnp.where(jrow < i, gate, -jnp.inf)
        sel = jnp.zeros(gate.shape, jnp.bool_)
        for r in range(MOBA_TOPK):
            mx = jnp.max(gate, axis=0, keepdims=True)
            idx = jnp.min(jnp.where(gate == mx, jrow, n_kb), axis=0, keepdims=True)
            pick = jrow == idx
            sel = sel | (pick & (i > r))
            gate = jnp.where(pick, -jnp.inf, gate)
        seln_ref[s] = jnp.where(sel, 0.0, -jnp.inf)
        st = lax.dot_general(k_ref[pl.ds(start, blk), hls[s]], qh, _NT, preferred_element_type=F32)
        st = jnp.where(causal, st, -jnp.inf)
        m0 = jnp.max(st, axis=0, keepdims=True)
        pt = jnp.exp(st - m0)
        l0 = jnp.sum(pt, axis=0, keepdims=True)
        acc0 = jnp.dot(vt_ref[i, hrs[s], :], pt.astype(BF16), preferred_element_type=F32)
        init.append((m0, l0, acc0))

    def group(g, carry):
        out = []
        for s in range(nh):
            m_old, l_old, acc_old = carry[s]
            qh = q_ref[:, hls[s]]
            sts, offs = [], []
            m_new = m_old
            for u in range(ATTN_GROUP):
                j = g * ATTN_GROUP + u
                ks = pl.multiple_of(j * blk, blk)
                st_u = lax.dot_general(k_ref[pl.ds(ks, blk), hls[s]], qh, _NT, preferred_element_type=F32)
                cj = slopes[s] * ((j - i) * blk).astype(F32)
                off = cj + seln_ref[s, pl.ds(j, 1), :]
                m_new = jnp.maximum(m_new, jnp.max(st_u, axis=0, keepdims=True) + off)
                sts.append(st_u)
                offs.append(off)
            alpha = jnp.exp(m_old - m_new)
            l_new = alpha * l_old
            acc_new = alpha * acc_old
            for u in range(ATTN_GROUP):
                j = g * ATTN_GROUP + u
                pt_u = jnp.exp(sts[u] - (m_new - offs[u]))
                l_new = l_new + jnp.sum(pt_u, axis=0, keepdims=True)
                acc_new = acc_new + jnp.dot(vt_ref[j, hrs[s], :], pt_u.astype(BF16), preferred_element_type=F32)
            out.append((m_new, l_new, acc_new))
        return tuple(out)

    final = lax.fori_loop(0, (i + ATTN_GROUP - 1) // ATTN_GROUP, group, tuple(init))
    ot = jnp.concatenate([acc_f / l_f for (_, l_f, acc_f) in final], axis=0)
    o_ref[...] = ot.T.astype(o_ref.dtype)


def _moba_prompt(slopes, qa, ka, vt, kmean):
    b, l, ap = qa.shape
    n_kb = l // MOBA_BLOCK
    blk = MOBA_BLOCK
    nh = ATTN_HEADS
    n_hg = ap // (nh * HEAD_PAD)
    assert n_kb % ATTN_GROUP == 0 and ap % (nh * HEAD_PAD) == 0
    grid_spec = pltpu.PrefetchScalarGridSpec(
        num_scalar_prefetch=1,
        grid=(b, n_hg, n_kb),
        in_specs=[pl.BlockSpec((None, blk, nh * HEAD_PAD), lambda bi, p, i, sl: (bi, i, p)),
                  pl.BlockSpec((None, l, nh * HEAD_PAD), lambda bi, p, i, sl: (bi, 0, p)),
                  pl.BlockSpec((None, n_kb, nh * HEAD_DIM, blk), lambda bi, p, i, sl: (bi, 0, p, 0)),
                  pl.BlockSpec((None, n_kb, nh * HEAD_PAD), lambda bi, p, i, sl: (bi, 0, p))],
        out_specs=pl.BlockSpec((None, blk, nh * HEAD_DIM), lambda bi, p, i, sl: (bi, i, p)),
        scratch_shapes=[pltpu.VMEM((nh, n_kb, blk), F32)])
    return pl.pallas_call(
        _attn_kernel,
        out_shape=jax.ShapeDtypeStruct((b, l, n_hg * nh * HEAD_DIM), BF16),
        grid_spec=grid_spec,
        compiler_params=_cparams(("arbitrary", "arbitrary", "arbitrary")),
        name="moba_prompt",
    )(slopes, qa, ka, vt, kmean)


def _mix_kernel(x_ref, attn_ref, u_ref, halo_ref, mod_ref, wp_ref, ps_ref, wo_ref, g_ref, wr_ref,
                xo_ref, h2_ref, lg_ref, ext_ref):
    t = pl.program_id(1)
    tm = x_ref.shape[0]
    a = attn_ref.shape[-1]
    halo = jnp.where(t == 0, 0.0, halo_ref[...])
    ext_ref[0:POOL_HALO, :] = halo
    ext_ref[POOL_HALO:POOL_HALO + tm, :] = u_ref[...]
    pos = t * tm + lax.broadcasted_iota(jnp.int32, (tm, 1), 0)
    gw = a // len(POOL_WINDOWS)
    pools = []
    for g, w in enumerate(POOL_WINDOWS):
        ln = slice(g * gw, (g + 1) * gw)
        acc = ext_ref[POOL_HALO:POOL_HALO + tm, ln]
        for s in range(1, w):
            acc = acc + ext_ref[POOL_HALO - s:POOL_HALO - s + tm, ln]
        cnt = jnp.minimum(w, pos + 1).astype(F32)
        diff = acc / cnt - u_ref[:, ln]
        pools.append(jnp.dot(diff.astype(BF16), wp_ref[g], preferred_element_type=F32))
    pool = jnp.concatenate(pools, axis=1) * ps_ref[...]
    mo = (jnp.dot(attn_ref[...], wo_ref[0:a, :], preferred_element_type=F32)
          + jnp.dot(pool.astype(BF16), wo_ref[a:, :], preferred_element_type=F32))
    xn = x_ref[...] + mod_ref[2:3, :] * mo
    xo_ref[...] = xn
    h2 = _rms(xn, g_ref[...]) * (1.0 + mod_ref[4:5, :]) + mod_ref[3:4, :]
    h2_ref[...] = h2
    lg_ref[...] = lax.dot_general(wr_ref[...], h2.astype(BF16), _NT, preferred_element_type=F32)


def _mix(x, attn, u, mod, w_pool_b, pool_scale, w_out_b, g_ffn, wr_b, tm):
    b, l, d = x.shape
    a = attn.shape[-1]
    ne = wr_b.shape[0]
    tok = lambda bi, t: (bi, t, 0)
    cst2 = lambda bi, t: (0, 0)
    hb = tm // POOL_HALO
    return pl.pallas_call(
        _mix_kernel,
        out_shape=(jax.ShapeDtypeStruct((b, l, d), F32),
                   jax.ShapeDtypeStruct((b, l, d), F32),
                   jax.ShapeDtypeStruct((b, ne, l), F32)),
        grid=(b, l // tm),
        in_specs=[pl.BlockSpec((None, tm, d), tok),
                  pl.BlockSpec((None, tm, a), tok),
                  pl.BlockSpec((None, tm, a), tok),
                  pl.BlockSpec((None, POOL_HALO, a), lambda bi, t: (bi, jnp.maximum(t * hb - 1, 0), 0)),
                  pl.BlockSpec((None, 6, d), lambda bi, t: (bi, 0, 0)),
                  pl.BlockSpec(w_pool_b.shape, lambda bi, t: (0, 0, 0)),
                  pl.BlockSpec((1, a), cst2),
                  pl.BlockSpec((d, d), cst2),
                  pl.BlockSpec((1, d), cst2),
                  pl.BlockSpec((ne, d), cst2)],
        out_specs=(pl.BlockSpec((None, tm, d), tok),
                   pl.BlockSpec((None, tm, d), tok),
                   pl.BlockSpec((None, ne, tm), lambda bi, t: (bi, 0, t))),
        scratch_shapes=[pltpu.VMEM((tm + POOL_HALO, a), F32)],
        compiler_params=_cparams(("arbitrary", "arbitrary")),
        name="mix",
    )(x, attn, u, u, mod, w_pool_b, pool_scale, w_out_b, g_ffn, wr_b)


def _moe_kernel(be_ref, na_ref, x_ref, w1_ref, w3_ref, w2_ref, o_ref):
    i = pl.program_id(0)

    @pl.when(i < na_ref[0])
    def _():
        x = x_ref[...].astype(BF16)
        a = jnp.dot(x, w1_ref[...], preferred_element_type=F32)
        g = jnp.dot(x, w3_ref[...], preferred_element_type=F32)
        hmid = (a * jax.nn.sigmoid(a)) * g
        o_ref[...] = jnp.dot(hmid.astype(BF16), w2_ref[...], preferred_element_type=F32)

    @pl.when(i >= na_ref[0])
    def _():
        o_ref[...] = jnp.zeros_like(o_ref)


def _moe_blocks(block_e, n_active, xb, w1_b, w3_b, w2_b):
    cap, d = xb.shape
    de = w1_b.shape[-1]
    n_blocks = cap // MOE_ROWS
    grid_spec = pltpu.PrefetchScalarGridSpec(
        num_scalar_prefetch=2,
        grid=(n_blocks,),
        in_specs=[pl.BlockSpec((MOE_ROWS, d), lambda i, be, na: (i, 0)),
                  pl.BlockSpec((None, d, de), lambda i, be, na: (be[i], 0, 0)),
                  pl.BlockSpec((None, d, de), lambda i, be, na: (be[i], 0, 0)),
                  pl.BlockSpec((None, de, d), lambda i, be, na: (be[i], 0, 0))],
        out_specs=pl.BlockSpec((MOE_ROWS, d), lambda i, be, na: (i, 0)))
    return pl.pallas_call(
        _moe_kernel,
        out_shape=jax.ShapeDtypeStruct((cap, d), F32),
        grid_spec=grid_spec,
        compiler_params=_cparams(("arbitrary",)),
        name="moe_blocks",
    )(block_e, n_active, xb, w1_b, w3_b, w2_b)


def _route_rows(lg, bias):
    n_e = lg.shape[0]
    epg = n_e // N_EXPERT_GROUPS
    sc = jax.nn.sigmoid(lg)
    bi = sc + bias
    s_rows = [sc[e:e + 1, :] for e in range(n_e)]
    b_rows = [bi[e:e + 1, :] for e in range(n_e)]
    gs = []
    for g in range(N_EXPERT_GROUPS):
        v = b_rows[g * epg:(g + 1) * epg]
        best = None
        for a_ in range(epg):
            for b_ in range(a_ + 1, epg):
                pr = v[a_] + v[b_]
                best = pr if best is None else jnp.maximum(best, pr)
        gs.append(best)
    grp = jnp.zeros(gs[0].shape, jnp.int32)
    gbest = gs[0]
    for g in range(1, N_EXPERT_GROUPS):
        up = gs[g] > gbest
        gbest = jnp.where(up, gs[g], gbest)
        grp = jnp.where(up, g, grp)
    c = [b_rows[j] for j in range(epg)]
    d = [s_rows[j] for j in range(epg)]
    for g in range(1, N_EXPERT_GROUPS):
        hit = grp == g
        c = [jnp.where(hit, b_rows[g * epg + j], c[j]) for j in range(epg)]
        d = [jnp.where(hit, s_rows[g * epg + j], d[j]) for j in range(epg)]

    def first_argmax(vals):
        idx = jnp.zeros(vals[0].shape, jnp.int32)
        best = vals[0]
        for j in range(1, len(vals)):
            up = vals[j] > best
            best = jnp.where(up, vals[j], best)
            idx = jnp.where(up, j, idx)
        return idx

    i1 = first_argmax(c)
    i2 = first_argmax([jnp.where(i1 == j, -jnp.inf, c[j]) for j in range(epg)])
    g1 = d[0]
    g2 = d[0]
    for j in range(1, epg):
        g1 = jnp.where(i1 == j, d[j], g1)
        g2 = jnp.where(i2 == j, d[j], g2)
    tot = g1 + g2
    return grp * epg + i1, grp * epg + i2, g1 / tot, g2 / tot


def _route_kernel(lg_ref, bias_ref, tri_ref, io_ref, fo_ref, cnt_ref, run_ref):
    first = (pl.program_id(0) == 0) & (pl.program_id(1) == 0)

    @pl.when(first)
    def _():
        run_ref[...] = jnp.zeros_like(run_ref)

    lg = lg_ref[...]
    n_e, tr = lg.shape
    e0, e1, g0, g1 = _route_rows(lg, bias_ref[...])
    erow = lax.broadcasted_iota(jnp.int32, (n_e, tr), 0)
    oh0 = erow == e0
    oh1 = erow == e1
    both = (oh0 | oh1).astype(F32)
    before = jnp.dot(both.astype(BF16), tri_ref[...], preferred_element_type=F32) + run_ref[...]
    r0 = jnp.sum(jnp.where(oh0, before, 0.0), axis=0, keepdims=True)
    r1 = jnp.sum(jnp.where(oh1, before, 0.0), axis=0, keepdims=True)
    run_ref[...] = run_ref[...] + jnp.sum(both, axis=1, keepdims=True)
    io_ref[...] = jnp.concatenate([e0, e1, r0.astype(jnp.int32), r1.astype(jnp.int32),
                                   jnp.zeros((4, tr), jnp.int32)], axis=0)
    fo_ref[...] = jnp.concatenate([g0, g1, jnp.zeros((6, tr), F32)], axis=0)
    cnt_ref[...] = jnp.broadcast_to(run_ref[...], cnt_ref.shape)


def _route(lgt, router_bias, tr):
    b, n_e, l = lgt.shape
    tri = (jnp.arange(tr)[:, None] < jnp.arange(tr)[None, :]).astype(BF16)
    io, fo, cnt = pl.pallas_call(
        _route_kernel,
        out_shape=(jax.ShapeDtypeStruct((b, 8, l), jnp.int32),
                   jax.ShapeDtypeStruct((b, 8, l), F32),
                   jax.ShapeDtypeStruct((n_e, LANES), F32)),
        grid=(b, l // tr),
        in_specs=[pl.BlockSpec((None, n_e, tr), lambda bi, t: (bi, 0, t)),
                  pl.BlockSpec((n_e, 1), lambda bi, t: (0, 0)),
                  pl.BlockSpec((tr, tr), lambda bi, t: (0, 0))],
        out_specs=(pl.BlockSpec((None, 8, tr), lambda bi, t: (bi, 0, t)),
                   pl.BlockSpec((None, 8, tr), lambda bi, t: (bi, 0, t)),
                   pl.BlockSpec((n_e, LANES), lambda bi, t: (0, 0))),
        scratch_shapes=[pltpu.VMEM((n_e, 1), F32)],
        compiler_params=_cparams(("arbitrary", "arbitrary")),
        name="route",
    )(lgt, router_bias.astype(F32).reshape(n_e, 1), tri)
    idx = jnp.stack([io[:, 0], io[:, 1]], axis=-1).reshape(b * l, TOP_K)
    rank = jnp.stack([io[:, 2], io[:, 3]], axis=-1).reshape(b * l, TOP_K)
    gate = jnp.stack([fo[:, 0], fo[:, 1]], axis=-1).reshape(b * l, TOP_K)
    return idx, rank, gate, cnt[:, 0].astype(jnp.int32)


def _row_copy(src_ref, src_row, dst_ref, dst_row, sem_ref):
    return pltpu.make_async_copy(src_ref.at[pl.ds(src_row, 1)], dst_ref.at[pl.ds(dst_row, 1)], sem_ref.at[0])


def _dispatch_kernel(dest_ref, h_ref, xz_ref, xb_ref, sem_ref):
    del xz_ref
    i = pl.program_id(0)
    td = dest_ref.shape[1] // TOP_K

    def issue(r, c):
        for k in range(TOP_K):
            _row_copy(h_ref, i * td + r, xb_ref, dest_ref[0, TOP_K * r + k], sem_ref).start()
        return c

    lax.fori_loop(0, td, issue, 0, unroll=8)

    def drain(r, c):
        _row_copy(h_ref, 0, xb_ref, 0, sem_ref).wait()
        return c

    lax.fori_loop(0, TOP_K * td, drain, 0, unroll=8)


def _dispatch(dest2, h2, cap):
    t, d = h2.shape
    n_steps = dest2.shape[0]
    anyspec = pl.BlockSpec(memory_space=pl.ANY)
    return pl.pallas_call(
        _dispatch_kernel,
        out_shape=jax.ShapeDtypeStruct((cap, d), h2.dtype),
        grid=(n_steps,),
        in_specs=[pl.BlockSpec((None, 1, dest2.shape[2]), lambda i: (i, 0, 0), memory_space=pltpu.SMEM), anyspec, anyspec],
        out_specs=anyspec,
        scratch_shapes=[pltpu.SemaphoreType.DMA((1,))],
        input_output_aliases={2: 0},
        compiler_params=_cparams(("arbitrary",)),
        name="moe_dispatch",
    )(dest2, h2, jnp.zeros((cap, d), h2.dtype))


def _combine_kernel(dest_ref, yb_ref, xn_ref, mod_ref, gate_ref, g_ref, o_ref, buf_ref, sem_ref, *, final):
    tc = xn_ref.shape[0]

    def issue(r, c):
        for k in range(TOP_K):
            pltpu.make_async_copy(yb_ref.at[pl.ds(dest_ref[0, TOP_K * r + k], 1)], buf_ref.at[k, pl.ds(r, 1)],
                                  sem_ref.at[0]).start()
        return c

    lax.fori_loop(0, tc, issue, 0, unroll=8)

    def drain(r, c):
        pltpu.make_async_copy(yb_ref.at[pl.ds(0, 1)], buf_ref.at[0, pl.ds(0, 1)], sem_ref.at[0]).wait()
        return c

    lax.fori_loop(0, TOP_K * tc, drain, 0, unroll=8)
    gate = gate_ref[...].astype(BF16).astype(F32)
    y = buf_ref[0].astype(BF16).astype(F32) * gate[:, 0:1]
    for k in range(1, TOP_K):
        y = y + buf_ref[k].astype(BF16).astype(F32) * gate[:, k:k + 1]
    x = xn_ref[...] + mod_ref[5:6, :] * y
    o_ref[...] = _rms(x, g_ref[...]) if final else x


def _combine(dest2, yb, xn, mod, gate, g_final, tc, final):
    b, l, d = xn.shape
    nt = l // tc
    anyspec = pl.BlockSpec(memory_space=pl.ANY)
    return pl.pallas_call(
        functools.partial(_combine_kernel, final=final),
        out_shape=jax.ShapeDtypeStruct((b, l, d), F32),
        grid=(b, nt),
        in_specs=[pl.BlockSpec((None, 1, TOP_K * tc), lambda bi, t: (bi * nt + t, 0, 0), memory_space=pltpu.SMEM),
                  anyspec,
                  pl.BlockSpec((None, tc, d), lambda bi, t: (bi, t, 0)),
                  pl.BlockSpec((None, 6, d), lambda bi, t: (bi, 0, 0)),
                  pl.BlockSpec((None, tc, TOP_K), lambda bi, t: (bi, t, 0)),
                  pl.BlockSpec((1, d), lambda bi, t: (0, 0))],
        out_specs=pl.BlockSpec((None, tc, d), lambda bi, t: (bi, t, 0)),
        scratch_shapes=[pltpu.VMEM((TOP_K, tc, d), F32), pltpu.SemaphoreType.DMA((1,))],
        compiler_params=_cparams(("arbitrary", "arbitrary")),
        name="moe_combine",
    )(dest2, yb, xn, mod, gate.reshape(b, l, TOP_K), g_final)


def _moe(h2, lgt, router_bias, w1_b, w3_b, w2_b, tr):
    t, d = h2.shape
    n_experts = w1_b.shape[0]
    idx, rank, gate, counts = _route(lgt, router_bias, tr)
    s = t * TOP_K
    padded = (counts + MOE_ROWS - 1) // MOE_ROWS * MOE_ROWS
    pad_end = jnp.cumsum(padded)
    pad_start = pad_end - padded
    onehot = idx[..., None] == jnp.arange(n_experts)
    dest = rank + jnp.sum(jnp.where(onehot, pad_start, 0), axis=-1)
    n_blocks = -(-s // MOE_ROWS) + n_experts
    cap = n_blocks * MOE_ROWS
    dest2 = dest.reshape(t // MOE_TOK, 1, MOE_TOK * TOP_K).astype(jnp.int32)
    xb = _dispatch(dest2, h2, cap)
    block_e = jnp.minimum(jnp.searchsorted(pad_end, jnp.arange(n_blocks) * MOE_ROWS, side='right'),
                          n_experts - 1).astype(jnp.int32)
    n_active = (pad_end[-1] // MOE_ROWS).astype(jnp.int32).reshape(1)
    yb = _moe_blocks(block_e, n_active, xb, w1_b, w3_b, w2_b)
    return yb, dest2, gate


def _final_kernel(x_ref, g_ref, o_ref):
    o_ref[...] = _rms(x_ref[...], g_ref[...])


def _final_norm(x2d, g):
    t, d = x2d.shape
    tm = min(t, 512)
    return pl.pallas_call(
        _final_kernel,
        out_shape=jax.ShapeDtypeStruct((t, d), F32),
        grid=(t // tm,),
        in_specs=[pl.BlockSpec((tm, d), lambda i: (i, 0)), pl.BlockSpec((1, d), lambda i: (0, 0))],
        out_specs=pl.BlockSpec((tm, d), lambda i: (i, 0)),
        compiler_params=_cparams(("arbitrary",)),
        name="final_norm",
    )(x2d, g)


def _alibi_slopes(n_heads):
    return 2.0 ** (-8.0 * (jnp.arange(n_heads, dtype=F32) + 1.0) / n_heads)


def _kmean_copy(cache_ref, buf_ref, sem_ref, pt_ref, slot, l, b, t, pg):
    phys = pt_ref[b, t * KMEAN_PAGES + pg]
    return pltpu.make_async_copy(cache_ref.at[l, phys], buf_ref.at[slot, pg], sem_ref.at[slot])


def _kmean_kernel(pt_ref, cache_ref, o_ref, buf_ref, sem_ref):
    l, b, t = pl.program_id(0), pl.program_id(1), pl.program_id(2)
    n_b, n_t = pl.num_programs(1), pl.num_programs(2)
    step = (l * n_b + b) * n_t + t
    slot = step % 2
    last = step == pl.num_programs(0) * n_b * n_t - 1

    @pl.when(step == 0)
    def _():
        for pg in range(KMEAN_PAGES):
            _kmean_copy(cache_ref, buf_ref, sem_ref, pt_ref, 0, l, b, t, pg).start()

    @pl.when(jnp.logical_not(last))
    def _():
        wrap_t = t + 1 == n_t
        wrap_b = wrap_t & (b + 1 == n_b)
        t2 = jnp.where(wrap_t, 0, t + 1)
        b2 = jnp.where(wrap_b, 0, jnp.where(wrap_t, b + 1, b))
        l2 = jnp.where(wrap_b, l + 1, l)
        for pg in range(KMEAN_PAGES):
            _kmean_copy(cache_ref, buf_ref, sem_ref, pt_ref, 1 - slot, l2, b2, t2, pg).start()

    for pg in range(KMEAN_PAGES):
        _kmean_copy(cache_ref, buf_ref, sem_ref, pt_ref, slot, l, b, t, pg).wait()
    ppb = MOBA_BLOCK // buf_ref.shape[2]
    for g in range(KMEAN_PAGES // ppb):
        tot = jnp.sum(buf_ref[slot, g * ppb], axis=0, keepdims=True)
        for q in range(1, ppb):
            tot = tot + jnp.sum(buf_ref[slot, g * ppb + q], axis=0, keepdims=True)
        o_ref[g:g + 1, :] = tot * (1.0 / MOBA_BLOCK)


def _kmean_sample(cache_k4, page_table, n_fp):
    depth, _, page, a = cache_k4.shape
    db = page_table.shape[0]
    ppb = MOBA_BLOCK // page
    gblk = KMEAN_PAGES // ppb
    assert n_fp % gblk == 0
    grid_spec = pltpu.PrefetchScalarGridSpec(
        num_scalar_prefetch=1,
        grid=(depth, db, n_fp // gblk),
        in_specs=[pl.BlockSpec(memory_space=pl.ANY)],
        out_specs=pl.BlockSpec((None, None, gblk, a), lambda l, b, t, pt: (l, b, t, 0)),
        scratch_shapes=[pltpu.VMEM((2, KMEAN_PAGES, page, a), F32), pltpu.SemaphoreType.DMA((2,))])
    return pl.pallas_call(
        _kmean_kernel,
        out_shape=jax.ShapeDtypeStruct((depth, db, n_fp, a), F32),
        grid_spec=grid_spec,
        compiler_params=_cparams(("arbitrary", "arbitrary", "arbitrary")),
        name="kmean_sample",
    )(page_table, cache_k4)


def _spre_kernel(x_ref, sh_ref, sc_ref, g_ref, w_ref, o_ref):
    h = _rms(x_ref[...], g_ref[...]) * (1.0 + sc_ref[...]) + sh_ref[...]
    o_ref[...] = jnp.dot(h.astype(BF16), w_ref[...], preferred_element_type=F32)


def _sample_pre(x, sh, sc, g, w_in_b):
    db, d = x.shape
    n = w_in_b.shape[1]
    full = lambda shape: pl.BlockSpec(shape, lambda i: (0,) * len(shape))
    return pl.pallas_call(
        _spre_kernel,
        out_shape=jax.ShapeDtypeStruct((db, n), F32),
        grid=(1,),
        in_specs=[full((db, d)), full((db, d)), full((db, d)), full((1, d)), full((d, n))],
        out_specs=full((db, n)),
        compiler_params=_cparams(("arbitrary",)),
        name="sample_pre",
    )(x, sh, sc, g, w_in_b)


def _sgate_kernel(q_ref, km_ref, o_ref):
    n_fp, a = km_ref.shape
    n_heads = a // HEAD_DIM
    hrow = lax.broadcasted_iota(jnp.int32, (n_heads, a), 0)
    lane = lax.broadcasted_iota(jnp.int32, (n_heads, a), 1)
    qrows = jnp.where(lane // HEAD_DIM == hrow, jnp.broadcast_to(q_ref[...], (n_heads, a)), 0.0).astype(BF16)
    gate = lax.dot_general(qrows, km_ref[...].astype(BF16), _NT, preferred_element_type=F32)
    jl = lax.broadcasted_iota(jnp.int32, gate.shape, 1)
    ol = lax.broadcasted_iota(jnp.int32, o_ref.shape, 1)
    out = jnp.zeros(o_ref.shape, jnp.int32)
    for r in range(MOBA_TOPK):
        mx = jnp.max(gate, axis=1, keepdims=True)
        idx = jnp.min(jnp.where(gate == mx, jl, n_fp), axis=1, keepdims=True)
        out = jnp.where(ol == r, idx, out)
        gate = jnp.where(jl == idx, -jnp.inf, gate)
    o_ref[...] = out


def _sample_gate(q3, kmean_l):
    db, _, a = q3.shape
    n_fp = kmean_l.shape[1]
    n_heads = a // HEAD_DIM
    return pl.pallas_call(
        _sgate_kernel,
        out_shape=jax.ShapeDtypeStruct((db, n_heads, LANES), jnp.int32),
        grid=(db,),
        in_specs=[pl.BlockSpec((None, 1, a), lambda b: (b, 0, 0)),
                  pl.BlockSpec((None, n_fp, a), lambda b: (b, 0, 0))],
        out_specs=pl.BlockSpec((None, n_heads, LANES), lambda b: (b, 0, 0)),
        compiler_params=_cparams(("arbitrary",)),
        name="sample_gate",
    )(q3, kmean_l)


def _sattn_copies(pt_ref, sel_ref, ck_ref, cv_ref, kb_ref, vb_ref, sem_ref, layer, slot, b, n_heads, ppb):
    out = []
    for h in range(n_heads):
        for r in range(MOBA_TOPK):
            blk = sel_ref[b, h * MOBA_TOPK + r]
            for pg in range(ppb):
                phys = pt_ref[b, blk * ppb + pg]
                i = r * ppb + pg
                out.append(pltpu.make_async_copy(ck_ref.at[layer, phys, :, h // 2, :], kb_ref.at[slot, h, i], sem_ref.at[slot]))
                out.append(pltpu.make_async_copy(cv_ref.at[layer, phys, :, h // 2, :], vb_ref.at[slot, h, i], sem_ref.at[slot]))
    return out


def _sattn_kernel(pt_ref, sel_ref, q_ref, k_ref, v_ref, ck_ref, cv_ref, o_ref, kb_ref, vb_ref, sem_ref, *, layer, past_len):
    b = pl.program_id(0)
    n_b = pl.num_programs(0)
    slot = b % 2
    n_heads, n_pg, page, pw = kb_ref.shape[1:]
    ppb = MOBA_BLOCK // page
    args = (pt_ref, sel_ref, ck_ref, cv_ref, kb_ref, vb_ref, sem_ref, layer)

    @pl.when(b == 0)
    def _():
        for c in _sattn_copies(*args, 0, b, n_heads, ppb):
            c.start()

    @pl.when(b + 1 < n_b)
    def _():
        for c in _sattn_copies(*args, 1 - slot, b + 1, n_heads, ppb):
            c.start()

    for c in _sattn_copies(*args, slot, b, n_heads, ppb):
        c.wait()

    n_sel = n_pg * page
    lane = lax.broadcasted_iota(jnp.int32, (1, pw), 1)
    kpos = lax.broadcasted_iota(jnp.int32, (1, n_sel), 1)
    outs = []
    for h in range(n_heads):
        pr = slice((h // 2) * pw, (h // 2 + 1) * pw)
        mine = (lane >= (h % 2) * HEAD_DIM) & (lane < (h % 2 + 1) * HEAD_DIM)
        slope = 2.0 ** (-8.0 * (h + 1.0) / n_heads)
        qf = jnp.where(mine, q_ref[:, pr] * HEAD_DIM ** -0.5, 0.0).astype(BF16)
        ksel = kb_ref[slot, h].reshape(n_sel, pw).astype(BF16)
        vsel = vb_ref[slot, h].reshape(n_sel, pw).astype(BF16)
        l_sel = lax.dot_general(jnp.broadcast_to(qf, (8, pw)), ksel, _NT, preferred_element_type=F32)[0:1]
        blk_of = jnp.zeros((1, n_sel), jnp.int32)
        for r in range(MOBA_TOPK):
            blk_of = jnp.where(kpos // MOBA_BLOCK == r, sel_ref[b, h * MOBA_TOPK + r], blk_of)
        pos_sel = blk_of * MOBA_BLOCK + kpos % MOBA_BLOCK
        l_sel = l_sel - slope * (past_len - pos_sel).astype(F32)
        kown = k_ref[:, pr].astype(BF16).astype(F32)
        vown = v_ref[:, pr].astype(BF16).astype(F32)
        l_own = jnp.sum(qf.astype(F32) * kown, axis=1, keepdims=True)
        mx = jnp.maximum(jnp.max(l_sel, axis=1, keepdims=True), l_own)
        e_sel = jnp.exp(l_sel - mx)
        e_own = jnp.exp(l_own - mx)
        den = jnp.sum(e_sel, axis=1, keepdims=True) + e_own
        p_sel = (e_sel / den).astype(BF16)
        p_own = (e_own / den).astype(BF16).astype(F32)
        o_h = jnp.dot(jnp.broadcast_to(p_sel, (8, n_sel)), vsel, preferred_element_type=F32)[0:1] + p_own * vown
        outs.append(jnp.where(mine, o_h, 0.0))
    o_ref[...] = jnp.concatenate([outs[2 * p] + outs[2 * p + 1] for p in range(n_heads // 2)], axis=1)


def _sample_attn(page_table, sel, q3, k3, v3, cache_k5, cache_v5, layer, past_len):
    db, _, a = q3.shape
    n_heads = a // HEAD_DIM
    page, pw = cache_k5.shape[2], cache_k5.shape[4]
    ppb = MOBA_BLOCK // page
    row = pl.BlockSpec((None, 1, a), lambda b, pt, sl: (b, 0, 0))
    anyspec = pl.BlockSpec(memory_space=pl.ANY)
    grid_spec = pltpu.PrefetchScalarGridSpec(
        num_scalar_prefetch=2,
        grid=(db,),
        in_specs=[row, row, row, anyspec, anyspec],
        out_specs=row,
        scratch_shapes=[pltpu.VMEM((2, n_heads, MOBA_TOPK * ppb, page, pw), F32),
                        pltpu.VMEM((2, n_heads, MOBA_TOPK * ppb, page, pw), F32),
                        pltpu.SemaphoreType.DMA((2,))])
    return pl.pallas_call(
        functools.partial(_sattn_kernel, layer=layer, past_len=past_len),
        out_shape=jax.ShapeDtypeStruct((db, 1, a), F32),
        grid_spec=grid_spec,
        compiler_params=_cparams(("arbitrary",)),
        name="sample_attn",
    )(page_table, sel, q3, k3, v3, cache_k5, cache_v5)


def _spost_kernel(x_ref, attn_ref, u_ref, st_ref, ga_ref, sh_ref, sc_ref, wp_ref, ps_ref, wo_ref, g_ref, wr_ref,
                  xo_ref, h2_ref, lg_ref, *, past_len):
    a = attn_ref.shape[-1]
    u = u_ref[...]
    st = st_ref[...]
    n_st = st.shape[1]
    gw = a // len(POOL_WINDOWS)
    srow = lax.broadcasted_iota(jnp.int32, (1, n_st, a), 1)
    lane = lax.broadcasted_iota(jnp.int32, (1, n_st, a), 2)
    win = jnp.zeros((1, n_st, a), jnp.int32)
    cnt = jnp.zeros((1, a), F32)
    lane2 = lax.broadcasted_iota(jnp.int32, (1, a), 1)
    for g, w in enumerate(POOL_WINDOWS):
        win = jnp.where(lane // gw == g, w, win)
        cnt = jnp.where(lane2 // gw == g, float(min(w, past_len + 1)), cnt)
    wsum = jnp.sum(jnp.where(srow >= n_st + 1 - win, st, 0.0), axis=1) + u
    diff = (wsum / cnt - u).astype(BF16)
    pools = [jnp.dot(diff[:, g * gw:(g + 1) * gw], wp_ref[g], preferred_element_type=F32)
             for g in range(len(POOL_WINDOWS))]
    pool = jnp.concatenate(pools, axis=1) * ps_ref[...]
    mo = (jnp.dot(attn_ref[...].astype(BF16), wo_ref[0:a, :], preferred_element_type=F32)
          + jnp.dot(pool.astype(BF16), wo_ref[a:, :], preferred_element_type=F32))
    xn = x_ref[...] + ga_ref[...] * mo
    xo_ref[...] = xn
    hb = (_rms(xn, g_ref[...]) * (1.0 + sc_ref[...]) + sh_ref[...]).astype(BF16)
    h2_ref[...] = hb
    lg_ref[...] = lax.dot_general(wr_ref[...], hb, _NT, preferred_element_type=F32)


def _sample_post(x, attn, u, state, ga1, sh2, sc2, w_pool_b, pool_scale, w_out_b, g_ffn, wr_b, past_len):
    db, d = x.shape
    ne = wr_b.shape[0]
    ins = (x, attn, u, state, ga1, sh2, sc2, w_pool_b, pool_scale, w_out_b, g_ffn, wr_b)
    full = lambda arr: pl.BlockSpec(arr.shape, lambda i, n=arr.ndim: (0,) * n)
    outs = (jax.ShapeDtypeStruct((db, d), F32), jax.ShapeDtypeStruct((db, d), BF16), jax.ShapeDtypeStruct((ne, db), F32))
    return pl.pallas_call(
        functools.partial(_spost_kernel, past_len=past_len),
        out_shape=outs,
        grid=(1,),
        in_specs=[full(x_) for x_ in ins],
        out_specs=tuple(full(o) for o in outs),
        compiler_params=_cparams(("arbitrary",)),
        name="sample_post",
    )(*ins)


def _smoe_kernel(h_ref, cb_ref, w1_ref, w3_ref, w2_ref, xn_ref, ga_ref, o_ref, acc_ref):
    e = pl.program_id(0)

    @pl.when(e == 0)
    def _():
        acc_ref[...] = jnp.zeros_like(acc_ref)

    h = h_ref[...]
    a = jnp.dot(h, w1_ref[...], preferred_element_type=F32)
    g = jnp.dot(h, w3_ref[...], preferred_element_type=F32)
    y = jnp.dot(((a * jax.nn.sigmoid(a)) * g).astype(BF16), w2_ref[...], preferred_element_type=F32)
    acc_ref[...] += y.astype(BF16).astype(F32) * cb_ref[...].astype(BF16).astype(F32)

    @pl.when(e == pl.num_programs(0) - 1)
    def _():
        o_ref[...] = xn_ref[...] + ga_ref[...] * acc_ref[...]


def _sample_moe(h2b, comb, w1_b, w3_b, w2_b, xn, ga2):
    db, d = h2b.shape
    n_e, _, de = w1_b.shape
    cst = lambda e: (0, 0)
    return pl.pallas_call(
        _smoe_kernel,
        out_shape=jax.ShapeDtypeStruct((db, d), F32),
        grid=(n_e,),
        in_specs=[pl.BlockSpec((db, d), cst),
                  pl.BlockSpec((None, db, 1), lambda e: (e, 0, 0)),
                  pl.BlockSpec((None, d, de), lambda e: (e, 0, 0)),
                  pl.BlockSpec((None, d, de), lambda e: (e, 0, 0)),
                  pl.BlockSpec((None, de, d), lambda e: (e, 0, 0)),
                  pl.BlockSpec((db, d), cst),
                  pl.BlockSpec((db, d), cst)],
        out_specs=pl.BlockSpec((db, d), cst),
        scratch_shapes=[pltpu.VMEM((db, d), F32)],
        compiler_params=_cparams(("arbitrary",)),
        name="sample_moe",
    )(h2b, comb, w1_b, w3_b, w2_b, xn, ga2)


def _sample_layer(x, mod_s, layer, kmean_l, cache_k5, cache_v5, page_table, state_l, g_mix_l, w_in_b_l, w_pool_b_l,
                  pool_scale_l, w_out_b_l, g_ffn_l, wr_b, router_bias, w1_b_l, w3_b_l, w2_b_l, past_len):
    db, d = x.shape
    a = kmean_l.shape[-1]
    n_e = w1_b_l.shape[0]
    sh1, sc1, ga1, sh2, sc2, ga2 = [mod_s[:, i] for i in range(6)]
    r = _sample_pre(x, sh1, sc1, g_mix_l, w_in_b_l)
    q3, k3, v3, u = r[:, None, :a], r[:, None, a:2 * a], r[:, None, 2 * a:3 * a], r[:, 3 * a:]
    sel = _sample_gate(q3, kmean_l)[:, :, :MOBA_TOPK].reshape(db, -1)
    attn = _sample_attn(page_table, sel, q3, k3, v3, cache_k5, cache_v5, layer, past_len)[:, 0]
    xn, h2b, lgt = _sample_post(x, attn, u, state_l, ga1, sh2, sc2, w_pool_b_l, pool_scale_l, w_out_b_l,
                                g_ffn_l, wr_b, past_len)
    idx, _, gate, _ = _route(lgt[None], router_bias, db)
    comb = jnp.sum(jnp.where(idx[None, :, :] == jnp.arange(n_e)[:, None, None], gate[None], 0.0), axis=-1)
    x_out = _sample_moe(h2b, comb[..., None], w1_b_l, w3_b_l, w2_b_l, xn, ga2)
    return x_out, k3, v3, u


def kernel(x_prompt, x_sample, cache_k, cache_v, state_pool, page_table, c_prompt, c_sample,
           w_ada, b_ada, g_mix, w_in, w_pool, pool_scale, w_out, g_ffn, w_router, router_bias,
           w1, w3, w2, g_final):
    b, l, d = x_prompt.shape
    db, ds, _ = x_sample.shape
    depth = w_ada.shape[0]
    n_heads = cache_k.shape[3]
    a = n_heads * HEAD_DIM
    n_experts = w1.shape[1]
    tm = 512
    assert l % tm == 0 and tm % MOBA_BLOCK == 0 and a % LANES == 0 and n_experts <= 16

    mod = _ada(jnp.concatenate([c_prompt, c_sample], axis=0), w_ada, b_ada)
    mod = mod.reshape(depth, b + db, 6, d)
    slopes = _alibi_slopes(n_heads)

    w_q, w_k, w_v = w_in[:, :, :a], w_in[:, :, a:2 * a], w_in[:, :, 2 * a:3 * a]

    def head_pad(w):
        w = w.reshape(depth, d, n_heads, HEAD_DIM)
        return jnp.pad(w, ((0, 0), (0, 0), (0, 0), (0, HEAD_PAD - HEAD_DIM))).reshape(depth, d, n_heads * HEAD_PAD)

    wn_b = w_in[:, :, a:].astype(BF16)
    wa_b = jnp.concatenate([head_pad(w_q), head_pad(w_k)], axis=-1).astype(BF16)
    wvt_b = jnp.swapaxes(w_v, 1, 2).astype(BF16)
    slopes_np = 2.0 ** (-8.0 * (np.arange(n_heads, dtype=np.float64) + 1.0) / n_heads)
    key_term = (slopes_np[:, None] * np.arange(MOBA_BLOCK)[None, :]).astype(np.float32)
    assert np.array_equal(key_term.astype(BF16).astype(np.float32), key_term), \
        "ALiBi key term must be exact in bfloat16 to ride in the padded key lanes"
    alibi_lane = np.zeros((n_heads, HEAD_PAD), np.float32)
    alibi_lane[:, HEAD_DIM] = 1.0
    orow = jnp.asarray(alibi_lane.reshape(1, -1))
    srow = jnp.asarray((alibi_lane * slopes_np[:, None].astype(np.float32)).reshape(1, -1))
    w_out_b = w_out.astype(BF16)
    w_pool_b = w_pool.astype(BF16)
    w1_b, w3_b, w2_b = w1.astype(BF16), w3.astype(BF16), w2.astype(BF16)
    wr_b = w_router.T.astype(BF16)

    w_in_b = w_in.astype(BF16)
    page = cache_k.shape[2]
    n_pool = cache_k.shape[1]
    past_len = page_table.shape[1] * page
    n_fp = past_len // MOBA_BLOCK
    assert ds == 1 and past_len % MOBA_BLOCK == 0 and n_fp >= MOBA_TOPK and MOBA_BLOCK % page == 0
    cache_k5 = cache_k.reshape(depth, n_pool, page, a // LANES, LANES)
    cache_v5 = cache_v.reshape(depth, n_pool, page, a // LANES, LANES)
    kmean_s = _kmean_sample(cache_k.reshape(depth, n_pool, page, a), page_table, n_fp)

    xp, xs = x_prompt, x_sample.reshape(db, d)
    kp, vp, up, kq, vq, uq = [], [], [], [], [], []
    for li in range(depth):
        mod_p = mod[li, :b]
        k, v, u, qa, ka, vt, kmean = _qkvu(xp, mod_p, g_mix[li][None], wn_b[li], wa_b[li], wvt_b[li], srow, orow, tm)
        attn = _moba_prompt(slopes, qa, ka, vt, kmean)
        xn, h2, lg = _mix(xp, attn, u, mod_p, w_pool_b[li], pool_scale[li][None], w_out_b[li],
                          g_ffn[li][None], wr_b, tm)
        yb, dest2, gate = _moe(h2.reshape(b * l, d), lg, router_bias, w1_b[li], w3_b[li], w2_b[li], tm)
        xp = _combine(dest2, yb, xn, mod_p, gate, g_final[None], MOE_TOK, li == depth - 1)
        kp.append(k.reshape(b, l, n_heads, HEAD_DIM))
        vp.append(v.reshape(b, l, n_heads, HEAD_DIM))
        up.append(u[:, l - POOL_STATE_LEN:])

        xs, k3, v3, u_s = _sample_layer(
            xs, mod[li, b:], li, kmean_s[li], cache_k5, cache_v5, page_table, state_pool[li], g_mix[li][None],
            w_in_b[li], w_pool_b[li], pool_scale[li][None], w_out_b[li], g_ffn[li][None], wr_b, router_bias,
            w1_b[li], w3_b[li], w2_b[li], past_len)
        kq.append(k3.reshape(db, ds, n_heads, HEAD_DIM))
        vq.append(v3.reshape(db, ds, n_heads, HEAD_DIM))
        uq.append(jnp.concatenate([state_pool[li][:, ds:], u_s[:, None]], axis=1))

    y_prompt = xp
    y_sample = _final_norm(xs, g_final[None]).reshape(db, ds, d)
    return (y_prompt, y_sample, jnp.stack(kp), jnp.stack(vp), jnp.stack(up),
            jnp.stack(kq), jnp.stack(vq), jnp.stack(uq))
```

```python
import functools

import numpy as np
import jax
import jax.numpy as jnp
from jax import lax
from jax.experimental import pallas as pl
from jax.experimental.pallas import tpu as pltpu

F32 = jnp.float32
BF16 = jnp.bfloat16

HEAD_DIM = 64
MOBA_BLOCK = 256
MOBA_TOPK = 3
POOL_WINDOWS = (2, 4, 8, 16)
POOL_STATE_LEN = max(POOL_WINDOWS) - 1
POOL_HALO = 16
N_EXPERT_GROUPS = 4
TOP_K = 2
NORM_EPS = 1e-6
LANES = 128
HEAD_PAD = 128
ATTN_GROUP = 4
ATTN_HEADS = 4
KMEAN_PAGES = 16
MOE_TOK = 256
MOE_ROWS = 256
VMEM_LIMIT = 48 * 1024 * 1024

_NT = (((1,), (1,)), ((), ()))
_TN = (((0,), (0,)), ((), ()))


def _cparams(sem):
    return pltpu.CompilerParams(dimension_semantics=sem, vmem_limit_bytes=VMEM_LIMIT)


def _rms(x, g):
    return x * lax.rsqrt(jnp.mean(x * x, axis=-1, keepdims=True) + NORM_EPS) * g


def _ada_kernel(c_ref, w_ref, b_ref, o_ref):
    c = c_ref[...]
    s = c * jax.nn.sigmoid(c)
    o_ref[...] = jnp.dot(s.astype(BF16), w_ref[...].astype(BF16), preferred_element_type=F32) + b_ref[...]


def _ada(c, w_ada, b_ada):
    depth, d, n = w_ada.shape
    nb = c.shape[0]
    tn = 1536
    return pl.pallas_call(
        _ada_kernel,
        out_shape=jax.ShapeDtypeStruct((depth, nb, n), F32),
        grid=(depth, n // tn),
        in_specs=[pl.BlockSpec((nb, d), lambda l, j: (0, 0)),
                  pl.BlockSpec((None, d, tn), lambda l, j: (l, 0, j)),
                  pl.BlockSpec((None, 1, tn), lambda l, j: (l, 0, j))],
        out_specs=pl.BlockSpec((None, nb, tn), lambda l, j: (l, 0, j)),
        compiler_params=_cparams(("arbitrary", "arbitrary")),
        name="ada",
    )(c, w_ada, b_ada.reshape(depth, 1, n))


def _qkvu_kernel(x_ref, mod_ref, g_ref, wn_ref, wa_ref, wvt_ref, srow_ref, orow_ref,
                 k_ref, v_ref, u_ref, qa_ref, ka_ref, vt_ref, km_ref):
    a = k_ref.shape[-1]
    tm = x_ref.shape[0]
    ap = qa_ref.shape[-1]
    h = _rms(x_ref[...], g_ref[...]) * (1.0 + mod_ref[1:2, :]) + mod_ref[0:1, :]
    hb = h.astype(BF16)
    rn = jnp.dot(hb, wn_ref[...], preferred_element_type=F32)
    k_ref[...] = rn[:, :a]
    v_ref[...] = rn[:, a:2 * a]
    u_ref[...] = rn[:, 2 * a:]
    ra = jnp.dot(hb, wa_ref[...], preferred_element_type=F32)
    qa_ref[...] = (ra[:, :ap] * HEAD_DIM ** -0.5 + orow_ref[...]).astype(BF16)
    kp = ra[:, ap:]
    ridx = (lax.broadcasted_iota(jnp.int32, (tm, 1), 0) % MOBA_BLOCK).astype(F32)
    ka_ref[...] = (kp + ridx * srow_ref[...]).astype(BF16)
    vt = lax.dot_general(wvt_ref[...], hb, _NT, preferred_element_type=F32)
    for s in range(km_ref.shape[0]):
        rows = slice(s * MOBA_BLOCK, (s + 1) * MOBA_BLOCK)
        km_ref[s:s + 1, :] = jnp.mean(kp[rows], axis=0, keepdims=True)
        vt_ref[s] = vt[:, rows].astype(BF16)


def _qkvu(x, mod, g, wn_b, wa_b, wvt_b, srow, orow, tm):
    b, l, d = x.shape
    a = wn_b.shape[1] // 3
    ap = wa_b.shape[1] // 2
    nkb_t = tm // MOBA_BLOCK
    tok = lambda bi, t: (bi, t, 0)
    cst = lambda bi, t: (0, 0)
    outs = pl.pallas_call(
        _qkvu_kernel,
        out_shape=(jax.ShapeDtypeStruct((b, l, a), F32),
                   jax.ShapeDtypeStruct((b, l, a), F32),
                   jax.ShapeDtypeStruct((b, l, a), F32),
                   jax.ShapeDtypeStruct((b, l, ap), BF16),
                   jax.ShapeDtypeStruct((b, l, ap), BF16),
                   jax.ShapeDtypeStruct((b, l // MOBA_BLOCK, a, MOBA_BLOCK), BF16),
                   jax.ShapeDtypeStruct((b, l // tm, nkb_t, ap), F32)),
        grid=(b, l // tm),
        in_specs=[pl.BlockSpec((None, tm, d), tok),
                  pl.BlockSpec((None, 6, d), lambda bi, t: (bi, 0, 0)),
                  pl.BlockSpec((1, d), cst),
                  pl.BlockSpec((d, 3 * a), cst),
                  pl.BlockSpec((d, 2 * ap), cst),
                  pl.BlockSpec((a, d), cst),
                  pl.BlockSpec((1, ap), cst),
                  pl.BlockSpec((1, ap), cst)],
        out_specs=(pl.BlockSpec((None, tm, a), tok),) * 3 + (pl.BlockSpec((None, tm, ap), tok),) * 2
        + (pl.BlockSpec((None, nkb_t, a, MOBA_BLOCK), lambda bi, t: (bi, t, 0, 0)),
           pl.BlockSpec((None, None, nkb_t, ap), lambda bi, t: (bi, t, 0, 0))),
        compiler_params=_cparams(("arbitrary", "arbitrary")),
        name="qkvu",
    )(x, mod, g, wn_b, wa_b, wvt_b, srow, orow)
    k, v, u, qa, ka, vt, km = outs
    return k, v, u, qa, ka, vt, km.reshape(b, l // MOBA_BLOCK, ap)


def _attn_kernel(sl_ref, q_ref, k_ref, vt_ref, km_ref, o_ref, seln_ref):
    p = pl.program_id(1)
    i = pl.program_id(2)
    blk = MOBA_BLOCK
    n_kb = km_ref.shape[0]
    nh = ATTN_HEADS
    kidx = lax.broadcasted_iota(jnp.int32, (blk, blk), 0)
    qidx = lax.broadcasted_iota(jnp.int32, (blk, blk), 1)
    causal = kidx <= qidx
    jrow = lax.broadcasted_iota(jnp.int32, (n_kb, blk), 0)
    start = pl.multiple_of(i * blk, blk)
    hls = [slice(s * HEAD_PAD, (s + 1) * HEAD_PAD) for s in range(nh)]
    hrs = [slice(s * HEAD_DIM, (s + 1) * HEAD_DIM) for s in range(nh)]
    slopes = [sl_ref[nh * p + s] for s in range(nh)]
    init = []
    for s in range(nh):
        qh = q_ref[:, hls[s]]
        gate = lax.dot_general(km_ref[:, hls[s]].astype(BF16), qh, _NT, preferred_element_type=F32)
        gate = jnp.where(jrow < i, gate, -jnp.inf)
        sel = jnp.zeros(gate.shape, jnp.bool_)
        for r in range(MOBA_TOPK):
            mx = jnp.max(gate, axis=0, keepdims=True)
            idx = jnp.min(jnp.where(gate == mx, jrow, n_kb), axis=0, keepdims=True)
            pick = jrow == idx
            sel = sel | (pick & (i > r))
            gate = jnp.where(pick, -jnp.inf, gate)
        seln_ref[s] = jnp.where(sel, 0.0, -jnp.inf)
        st = lax.dot_general(k_ref[pl.ds(start, blk), hls[s]], qh, _NT, preferred_element_type=F32)
        st = jnp.where(causal, st, -jnp.inf)
        m0 = jnp.max(st, axis=0, keepdims=True)
        pt = jnp.exp(st - m0)
        l0 = jnp.sum(pt, axis=0, keepdims=True)
        acc0 = jnp.dot(vt_ref[i, hrs[s], :], pt.astype(BF16), preferred_element_type=F32)
        init.append((m0, l0, acc0))

    def group(g, carry):
        out = []
        for s in range(nh):
            m_old, l_old, acc_old = carry[s]
            qh = q_ref[:, hls[s]]
            sts, offs = [], []
            m_new = m_old
            for u in range(ATTN_GROUP):
                j = g * ATTN_GROUP + u
                ks = pl.multiple_of(j * blk, blk)
                st_u = lax.dot_general(k_ref[pl.ds(ks, blk), hls[s]], qh, _NT, preferred_element_type=F32)
                cj = slopes[s] * ((j - i) * blk).astype(F32)
                off = cj + seln_ref[s, pl.ds(j, 1), :]
                m_new = jnp.maximum(m_new, jnp.max(st_u, axis=0, keepdims=True) + off)
                sts.append(st_u)
                offs.append(off)
            alpha = jnp.exp(m_old - m_new)
            l_new = alpha * l_old
            acc_new = alpha * acc_old
            for u in range(ATTN_GROUP):
                j = g * ATTN_GROUP + u
                pt_u = jnp.exp(sts[u] - (m_new - offs[u]))
                l_new = l_new + jnp.sum(pt_u, axis=0, keepdims=True)
                acc_new = acc_new + jnp.dot(vt_ref[j, hrs[s], :], pt_u.astype(BF16), preferred_element_type=F32)
            out.append((m_new, l_new, acc_new))
        return tuple(out)

    final = lax.fori_loop(0, (i + ATTN_GROUP - 1) // ATTN_GROUP, group, tuple(init))
    ot = jnp.concatenate([acc_f / l_f for (_, l_f, acc_f) in final], axis=0)
    o_ref[...] = ot.T.astype(o_ref.dtype)


def _moba_prompt(slopes, qa, ka, vt, kmean):
    b, l, ap = qa.shape
    n_kb = l // MOBA_BLOCK
    blk = MOBA_BLOCK
    nh = ATTN_HEADS
    n_hg = ap // (nh * HEAD_PAD)
    assert n_kb % ATTN_GROUP == 0 and ap % (nh * HEAD_PAD) == 0
    grid_spec = pltpu.PrefetchScalarGridSpec(
        num_scalar_prefetch=1,
        grid=(b, n_hg, n_kb),
        in_specs=[pl.BlockSpec((None, blk, nh * HEAD_PAD), lambda bi, p, i, sl: (bi, i, p)),
                  pl.BlockSpec((None, l, nh * HEAD_PAD), lambda bi, p, i, sl: (bi, 0, p)),
                  pl.BlockSpec((None, n_kb, nh * HEAD_DIM, blk), lambda bi, p, i, sl: (bi, 0, p, 0)),
                  pl.BlockSpec((None, n_kb, nh * HEAD_PAD), lambda bi, p, i, sl: (bi, 0, p))],
        out_specs=pl.BlockSpec((None, blk, nh * HEAD_DIM), lambda bi, p, i, sl: (bi, i, p)),
        scratch_shapes=[pltpu.VMEM((nh, n_kb, blk), F32)])
    return pl.pallas_call(
        _attn_kernel,
        out_shape=jax.ShapeDtypeStruct((b, l, n_hg * nh * HEAD_DIM), BF16),
        grid_spec=grid_spec,
        compiler_params=_cparams(("arbitrary", "arbitrary", "arbitrary")),
        name="moba_prompt",
    )(slopes, qa, ka, vt, kmean)


def _mix_kernel(x_ref, attn_ref, u_ref, halo_ref, mod_ref, wp_ref, ps_ref, wo_ref, g_ref, wr_ref,
                xo_ref, h2_ref, lg_ref, ext_ref):
    t = pl.program_id(1)
    tm = x_ref.shape[0]
    a = attn_ref.shape[-1]
    halo = jnp.where(t == 0, 0.0, halo_ref[...])
    ext_ref[0:POOL_HALO, :] = halo
    ext_ref[POOL_HALO:POOL_HALO + tm, :] = u_ref[...]
    pos = t * tm + lax.broadcasted_iota(jnp.int32, (tm, 1), 0)
    gw = a // len(POOL_WINDOWS)
    pools = []
    for g, w in enumerate(POOL_WINDOWS):
        ln = slice(g * gw, (g + 1) * gw)
        acc = ext_ref[POOL_HALO:POOL_HALO + tm, ln]
        for s in range(1, w):
            acc = acc + ext_ref[POOL_HALO - s:POOL_HALO - s + tm, ln]
        cnt = jnp.minimum(w, pos + 1).astype(F32)
        diff = acc / cnt - u_ref[:, ln]
        pools.append(jnp.dot(diff.astype(BF16), wp_ref[g], preferred_element_type=F32))
    pool = jnp.concatenate(pools, axis=1) * ps_ref[...]
    mo = (jnp.dot(attn_ref[...], wo_ref[0:a, :], preferred_element_type=F32)
          + jnp.dot(pool.astype(BF16), wo_ref[a:, :], preferred_element_type=F32))
    xn = x_ref[...] + mod_ref[2:3, :] * mo
    xo_ref[...] = xn
    h2 = _rms(xn, g_ref[...]) * (1.0 + mod_ref[4:5, :]) + mod_ref[3:4, :]
    h2_ref[...] = h2
    lg_ref[...] = lax.dot_general(wr_ref[...], h2.astype(BF16), _NT, preferred_element_type=F32)


def _mix(x, attn, u, mod, w_pool_b, pool_scale, w_out_b, g_ffn, wr_b, tm):
    b, l, d = x.shape
    a = attn.shape[-1]
    ne = wr_b.shape[0]
    tok = lambda bi, t: (bi, t, 0)
    cst2 = lambda bi, t: (0, 0)
    hb = tm // POOL_HALO
    return pl.pallas_call(
        _mix_kernel,
        out_shape=(jax.ShapeDtypeStruct((b, l, d), F32),
                   jax.ShapeDtypeStruct((b, l, d), F32),
                   jax.ShapeDtypeStruct((b, ne, l), F32)),
        grid=(b, l // tm),
        in_specs=[pl.BlockSpec((None, tm, d), tok),
                  pl.BlockSpec((None, tm, a), tok),
                  pl.BlockSpec((None, tm, a), tok),
                  pl.BlockSpec((None, POOL_HALO, a), lambda bi, t: (bi, jnp.maximum(t * hb - 1, 0), 0)),
                  pl.BlockSpec((None, 6, d), lambda bi, t: (bi, 0, 0)),
                  pl.BlockSpec(w_pool_b.shape, lambda bi, t: (0, 0, 0)),
                  pl.BlockSpec((1, a), cst2),
                  pl.BlockSpec((d, d), cst2),
                  pl.BlockSpec((1, d), cst2),
                  pl.BlockSpec((ne, d), cst2)],
        out_specs=(pl.BlockSpec((None, tm, d), tok),
                   pl.BlockSpec((None, tm, d), tok),
                   pl.BlockSpec((None, ne, tm), lambda bi, t: (bi, 0, t))),
        scratch_shapes=[pltpu.VMEM((tm + POOL_HALO, a), F32)],
        compiler_params=_cparams(("arbitrary", "arbitrary")),
        name="mix",
    )(x, attn, u, u, mod, w_pool_b, pool_scale, w_out_b, g_ffn, wr_b)


def _moe_kernel(be_ref, na_ref, x_ref, w1_ref, w3_ref, w2_ref, o_ref):
    i = pl.program_id(0)

    @pl.when(i < na_ref[0])
    def _():
        x = x_ref[...].astype(BF16)
        a = jnp.dot(x, w1_ref[...], preferred_element_type=F32)
        g = jnp.dot(x, w3_ref[...], preferred_element_type=F32)
        hmid = (a * jax.nn.sigmoid(a)) * g
        o_ref[...] = jnp.dot(hmid.astype(BF16), w2_ref[...], preferred_element_type=F32)

    @pl.when(i >= na_ref[0])
    def _():
        o_ref[...] = jnp.zeros_like(o_ref)


def _moe_blocks(block_e, n_active, xb, w1_b, w3_b, w2_b):
    cap, d = xb.shape
    de = w1_b.shape[-1]
    n_blocks = cap // MOE_ROWS
    grid_spec = pltpu.PrefetchScalarGridSpec(
        num_scalar_prefetch=2,
        grid=(n_blocks,),
        in_specs=[pl.BlockSpec((MOE_ROWS, d), lambda i, be, na: (i, 0)),
                  pl.BlockSpec((None, d, de), lambda i, be, na: (be[i], 0, 0)),
                  pl.BlockSpec((None, d, de), lambda i, be, na: (be[i], 0, 0)),
                  pl.BlockSpec((None, de, d), lambda i, be, na: (be[i], 0, 0))],
        out_specs=pl.BlockSpec((MOE_ROWS, d), lambda i, be, na: (i, 0)))
    return pl.pallas_call(
        _moe_kernel,
        out_shape=jax.ShapeDtypeStruct((cap, d), F32),
        grid_spec=grid_spec,
        compiler_params=_cparams(("arbitrary",)),
        name="moe_blocks",
    )(block_e, n_active, xb, w1_b, w3_b, w2_b)


def _route_rows(lg, bias):
    n_e = lg.shape[0]
    epg = n_e // N_EXPERT_GROUPS
    sc = jax.nn.sigmoid(lg)
    bi = sc + bias
    s_rows = [sc[e:e + 1, :] for e in range(n_e)]
    b_rows = [bi[e:e + 1, :] for e in range(n_e)]
    gs = []
    for g in range(N_EXPERT_GROUPS):
        v = b_rows[g * epg:(g + 1) * epg]
        best = None
        for a_ in range(epg):
            for b_ in range(a_ + 1, epg):
                pr = v[a_] + v[b_]
                best = pr if best is None else jnp.maximum(best, pr)
        gs.append(best)
    grp = jnp.zeros(gs[0].shape, jnp.int32)
    gbest = gs[0]
    for g in range(1, N_EXPERT_GROUPS):
        up = gs[g] > gbest
        gbest = jnp.where(up, gs[g], gbest)
        grp = jnp.where(up, g, grp)
    c = [b_rows[j] for j in range(epg)]
    d = [s_rows[j] for j in range(epg)]
    for g in range(1, N_EXPERT_GROUPS):
        hit = grp == g
        c = [jnp.where(hit, b_rows[g * epg + j], c[j]) for j in range(epg)]
        d = [jnp.where(hit, s_rows[g * epg + j], d[j]) for j in range(epg)]

    def first_argmax(vals):
        idx = jnp.zeros(vals[0].shape, jnp.int32)
        best = vals[0]
        for j in range(1, len(vals)):
            up = vals[j] > best
            best = jnp.where(up, vals[j], best)
            idx = jnp.where(up, j, idx)
        return idx

    i1 = first_argmax(c)
    i2 = first_argmax([jnp.where(i1 == j, -jnp.inf, c[j]) for j in range(epg)])
    g1 = d[0]
    g2 = d[0]
    for j in range(1, epg):
        g1 = jnp.where(i1 == j, d[j], g1)
        g2 = jnp.where(i2 == j, d[j], g2)
    tot = g1 + g2
    return grp * epg + i1, grp * epg + i2, g1 / tot, g2 / tot


def _route_kernel(lg_ref, bias_ref, tri_ref, io_ref, fo_ref, cnt_ref, run_ref):
    first = (pl.program_id(0) == 0) & (pl.program_id(1) == 0)

    @pl.when(first)
    def _():
        run_ref[...] = jnp.zeros_like(run_ref)

    lg = lg_ref[...]
    n_e, tr = lg.shape
    e0, e1, g0, g1 = _route_rows(lg, bias_ref[...])
    erow = lax.broadcasted_iota(jnp.int32, (n_e, tr), 0)
    oh0 = erow == e0
    oh1 = erow == e1
    both = (oh0 | oh1).astype(F32)
    before = jnp.dot(both.astype(BF16), tri_ref[...], preferred_element_type=F32) + run_ref[...]
    r0 = jnp.sum(jnp.where(oh0, before, 0.0), axis=0, keepdims=True)
    r1 = jnp.sum(jnp.where(oh1, before, 0.0), axis=0, keepdims=True)
    run_ref[...] = run_ref[...] + jnp.sum(both, axis=1, keepdims=True)
    io_ref[...] = jnp.concatenate([e0, e1, r0.astype(jnp.int32), r1.astype(jnp.int32),
                                   jnp.zeros((4, tr), jnp.int32)], axis=0)
    fo_ref[...] = jnp.concatenate([g0, g1, jnp.zeros((6, tr), F32)], axis=0)
    cnt_ref[...] = jnp.broadcast_to(run_ref[...], cnt_ref.shape)


def _route(lgt, router_bias, tr):
    b, n_e, l = lgt.shape
    tri = (jnp.arange(tr)[:, None] < jnp.arange(tr)[None, :]).astype(BF16)
    io, fo, cnt = pl.pallas_call(
        _route_kernel,
        out_shape=(jax.ShapeDtypeStruct((b, 8, l), jnp.int32),
                   jax.ShapeDtypeStruct((b, 8, l), F32),
                   jax.ShapeDtypeStruct((n_e, LANES), F32)),
        grid=(b, l // tr),
        in_specs=[pl.BlockSpec((None, n_e, tr), lambda bi, t: (bi, 0, t)),
                  pl.BlockSpec((n_e, 1), lambda bi, t: (0, 0)),
                  pl.BlockSpec((tr, tr), lambda bi, t: (0, 0))],
        out_specs=(pl.BlockSpec((None, 8, tr), lambda bi, t: (bi, 0, t)),
                   pl.BlockSpec((None, 8, tr), lambda bi, t: (bi, 0, t)),
                   pl.BlockSpec((n_e, LANES), lambda bi, t: (0, 0))),
        scratch_shapes=[pltpu.VMEM((n_e, 1), F32)],
        compiler_params=_cparams(("arbitrary", "arbitrary")),
        name="route",
    )(lgt, router_bias.astype(F32).reshape(n_e, 1), tri)
    idx = jnp.stack([io[:, 0], io[:, 1]], axis=-1).reshape(b * l, TOP_K)
    rank = jnp.stack([io[:, 2], io[:, 3]], axis=-1).reshape(b * l, TOP_K)
    gate = jnp.stack([fo[:, 0], fo[:, 1]], axis=-1).reshape(b * l, TOP_K)
    return idx, rank, gate, cnt[:, 0].astype(jnp.int32)


def _row_copy(src_ref, src_row, dst_ref, dst_row, sem_ref):
    return pltpu.make_async_copy(src_ref.at[pl.ds(src_row, 1)], dst_ref.at[pl.ds(dst_row, 1)], sem_ref.at[0])


def _dispatch_kernel(dest_ref, h_ref, xz_ref, xb_ref, sem_ref):
    del xz_ref
    td = dest_ref.shape[1] // TOP_K

    def issue(r, c):
        for k in range(TOP_K):
            _row_copy(h_ref, r, xb_ref, dest_ref[0, TOP_K * r + k], sem_ref).start()
        return c

    lax.fori_loop(0, td, issue, 0, unroll=8)

    def drain(r, c):
        _row_copy(h_ref, 0, xb_ref, 0, sem_ref).wait()
        return c

    lax.fori_loop(0, TOP_K * td, drain, 0, unroll=8)


def _dispatch(dest2, h2, cap):
    t, d = h2.shape
    n_steps = dest2.shape[0]
    anyspec = pl.BlockSpec(memory_space=pl.ANY)
    return pl.pallas_call(
        _dispatch_kernel,
        out_shape=jax.ShapeDtypeStruct((cap, d), h2.dtype),
        grid=(n_steps,),
        in_specs=[pl.BlockSpec((None, 1, dest2.shape[2]), lambda i: (i, 0, 0), memory_space=pltpu.SMEM),
                  pl.BlockSpec((dest2.shape[2] // TOP_K, d), lambda i: (i, 0)), anyspec],
        out_specs=anyspec,
        scratch_shapes=[pltpu.SemaphoreType.DMA((1,))],
        input_output_aliases={2: 0},
        compiler_params=_cparams(("arbitrary",)),
        name="moe_dispatch",
    )(dest2, h2, jnp.zeros((cap, d), h2.dtype))


def _combine_kernel(dest_ref, yb_ref, xn_ref, mod_ref, gate_ref, g_ref, o_ref, buf_ref, sem_ref, *, final):
    tc = xn_ref.shape[0]

    def issue(r, c):
        for k in range(TOP_K):
            pltpu.make_async_copy(yb_ref.at[pl.ds(dest_ref[0, TOP_K * r + k], 1)], buf_ref.at[k, pl.ds(r, 1)],
                                  sem_ref.at[0]).start()
        return c

    lax.fori_loop(0, tc, issue, 0, unroll=8)

    def drain(r, c):
        pltpu.make_async_copy(yb_ref.at[pl.ds(0, 1)], buf_ref.at[0, pl.ds(0, 1)], sem_ref.at[0]).wait()
        return c

    lax.fori_loop(0, TOP_K * tc, drain, 0, unroll=8)
    gate = gate_ref[...].astype(BF16).astype(F32)
    y = buf_ref[0].astype(BF16).astype(F32) * gate[:, 0:1]
    for k in range(1, TOP_K):
        y = y + buf_ref[k].astype(BF16).astype(F32) * gate[:, k:k + 1]
    x = xn_ref[...] + mod_ref[5:6, :] * y
    o_ref[...] = _rms(x, g_ref[...]) if final else x


def _combine(dest2, yb, xn, mod, gate, g_final, tc, final):
    b, l, d = xn.shape
    nt = l // tc
    anyspec = pl.BlockSpec(memory_space=pl.ANY)
    return pl.pallas_call(
        functools.partial(_combine_kernel, final=final),
        out_shape=jax.ShapeDtypeStruct((b, l, d), F32),
        grid=(b, nt),
        in_specs=[pl.BlockSpec((None, 1, TOP_K * tc), lambda bi, t: (bi * nt + t, 0, 0), memory_space=pltpu.SMEM),
                  anyspec,
                  pl.BlockSpec((None, tc, d), lambda bi, t: (bi, t, 0)),
                  pl.BlockSpec((None, 6, d), lambda bi, t: (bi, 0, 0)),
                  pl.BlockSpec((None, tc, TOP_K), lambda bi, t: (bi, t, 0)),
                  pl.BlockSpec((1, d), lambda bi, t: (0, 0))],
        out_specs=pl.BlockSpec((None, tc, d), lambda bi, t: (bi, t, 0)),
        scratch_shapes=[pltpu.VMEM((TOP_K, tc, d), F32), pltpu.SemaphoreType.DMA((1,))],
        compiler_params=_cparams(("arbitrary", "arbitrary")),
        name="moe_combine",
    )(dest2, yb, xn, mod, gate.reshape(b, l, TOP_K), g_final)


def _moe(h2, lgt, router_bias, w1_b, w3_b, w2_b, tr):
    t, d = h2.shape
    n_experts = w1_b.shape[0]
    idx, rank, gate, counts = _route(lgt, router_bias, tr)
    s = t * TOP_K
    padded = (counts + MOE_ROWS - 1) // MOE_ROWS * MOE_ROWS
    pad_end = jnp.cumsum(padded)
    pad_start = pad_end - padded
    onehot = idx[..., None] == jnp.arange(n_experts)
    dest = rank + jnp.sum(jnp.where(onehot, pad_start, 0), axis=-1)
    n_blocks = -(-s // MOE_ROWS) + n_experts
    cap = n_blocks * MOE_ROWS
    dest2 = dest.reshape(t // MOE_TOK, 1, MOE_TOK * TOP_K).astype(jnp.int32)
    xb = _dispatch(dest2, h2, cap)
    block_e = jnp.minimum(jnp.searchsorted(pad_end, jnp.arange(n_blocks) * MOE_ROWS, side='right'),
                          n_experts - 1).astype(jnp.int32)
    n_active = (pad_end[-1] // MOE_ROWS).astype(jnp.int32).reshape(1)
    yb = _moe_blocks(block_e, n_active, xb, w1_b, w3_b, w2_b)
    return yb, dest2, gate


def _final_kernel(x_ref, g_ref, o_ref):
    o_ref[...] = _rms(x_ref[...], g_ref[...])


def _final_norm(x2d, g):
    t, d = x2d.shape
    tm = min(t, 512)
    return pl.pallas_call(
        _final_kernel,
        out_shape=jax.ShapeDtypeStruct((t, d), F32),
        grid=(t // tm,),
        in_specs=[pl.BlockSpec((tm, d), lambda i: (i, 0)), pl.BlockSpec((1, d), lambda i: (0, 0))],
        out_specs=pl.BlockSpec((tm, d), lambda i: (i, 0)),
        compiler_params=_cparams(("arbitrary",)),
        name="final_norm",
    )(x2d, g)


def _alibi_slopes(n_heads):
    return 2.0 ** (-8.0 * (jnp.arange(n_heads, dtype=F32) + 1.0) / n_heads)


def _kmean_copy(cache_ref, buf_ref, sem_ref, pt_ref, slot, l, b, t, pg):
    phys = pt_ref[b, t * KMEAN_PAGES + pg]
    return pltpu.make_async_copy(cache_ref.at[l, phys], buf_ref.at[slot, pg], sem_ref.at[slot])


def _kmean_kernel(pt_ref, cache_ref, o_ref, buf_ref, sem_ref):
    l, b, t = pl.program_id(0), pl.program_id(1), pl.program_id(2)
    n_b, n_t = pl.num_programs(1), pl.num_programs(2)
    step = (l * n_b + b) * n_t + t
    slot = step % 2
    last = step == pl.num_programs(0) * n_b * n_t - 1

    @pl.when(step == 0)
    def _():
        for pg in range(KMEAN_PAGES):
            _kmean_copy(cache_ref, buf_ref, sem_ref, pt_ref, 0, l, b, t, pg).start()

    @pl.when(jnp.logical_not(last))
    def _():
        wrap_t = t + 1 == n_t
        wrap_b = wrap_t & (b + 1 == n_b)
        t2 = jnp.where(wrap_t, 0, t + 1)
        b2 = jnp.where(wrap_b, 0, jnp.where(wrap_t, b + 1, b))
        l2 = jnp.where(wrap_b, l + 1, l)
        for pg in range(KMEAN_PAGES):
            _kmean_copy(cache_ref, buf_ref, sem_ref, pt_ref, 1 - slot, l2, b2, t2, pg).start()

    for pg in range(KMEAN_PAGES):
        _kmean_copy(cache_ref, buf_ref, sem_ref, pt_ref, slot, l, b, t, pg).wait()

    @pl.when(t == 0)
    def _():
        o_ref[...] = jnp.zeros_like(o_ref)

    ppb = MOBA_BLOCK // buf_ref.shape[3]
    gblk = KMEAN_PAGES // ppb
    lane = lax.broadcasted_iota(jnp.int32, o_ref.shape, 1)
    acc = o_ref[...]
    for g in range(gblk):
        tot = buf_ref[slot, g * ppb]
        for q in range(1, ppb):
            tot = tot + buf_ref[slot, g * ppb + q]
        col = jnp.sum(tot, axis=1, keepdims=True) * (1.0 / MOBA_BLOCK)
        acc = jnp.where(lane == t * gblk + g, col, acc)
    o_ref[...] = acc


def _kmean_sample(cache_kt, page_table, n_fp):
    depth, _, a, page = cache_kt.shape
    db = page_table.shape[0]
    ppb = MOBA_BLOCK // page
    gblk = KMEAN_PAGES // ppb
    assert n_fp % gblk == 0
    grid_spec = pltpu.PrefetchScalarGridSpec(
        num_scalar_prefetch=1,
        grid=(depth, db, n_fp // gblk),
        in_specs=[pl.BlockSpec(memory_space=pl.ANY)],
        out_specs=pl.BlockSpec((None, None, a, n_fp), lambda l, b, t, pt: (l, b, 0, 0)),
        scratch_shapes=[pltpu.VMEM((2, KMEAN_PAGES, a, page), F32), pltpu.SemaphoreType.DMA((2,))])
    return pl.pallas_call(
        _kmean_kernel,
        out_shape=jax.ShapeDtypeStruct((depth, db, a, n_fp), F32),
        grid_spec=grid_spec,
        compiler_params=_cparams(("arbitrary", "arbitrary", "arbitrary")),
        name="kmean_sample",
    )(page_table, cache_kt)


def _spre_kernel(x_ref, sh_ref, sc_ref, g_ref, w_ref, o_ref):
    h = _rms(x_ref[...], g_ref[...]) * (1.0 + sc_ref[...]) + sh_ref[...]
    o_ref[...] = jnp.dot(h.astype(BF16), w_ref[...], preferred_element_type=F32)


def _sample_pre(x, sh, sc, g, w_in_b):
    db, d = x.shape
    n = w_in_b.shape[1]
    full = lambda shape: pl.BlockSpec(shape, lambda i: (0,) * len(shape))
    return pl.pallas_call(
        _spre_kernel,
        out_shape=jax.ShapeDtypeStruct((db, n), F32),
        grid=(1,),
        in_specs=[full((db, d)), full((db, d)), full((db, d)), full((1, d)), full((d, n))],
        out_specs=full((db, n)),
        compiler_params=_cparams(("arbitrary",)),
        name="sample_pre",
    )(x, sh, sc, g, w_in_b)


def _sgate_kernel(q_ref, km_ref, o_ref):
    a, n_fp = km_ref.shape
    n_heads = a // HEAD_DIM
    hrow = lax.broadcasted_iota(jnp.int32, (n_heads, a), 0)
    lane = lax.broadcasted_iota(jnp.int32, (n_heads, a), 1)
    qrows = jnp.where(lane // HEAD_DIM == hrow, jnp.broadcast_to(q_ref[...], (n_heads, a)), 0.0).astype(BF16)
    gate = jnp.dot(qrows, km_ref[...].astype(BF16), preferred_element_type=F32)
    jl = lax.broadcasted_iota(jnp.int32, gate.shape, 1)
    ol = lax.broadcasted_iota(jnp.int32, o_ref.shape, 1)
    out = jnp.zeros(o_ref.shape, jnp.int32)
    for r in range(MOBA_TOPK):
        mx = jnp.max(gate, axis=1, keepdims=True)
        idx = jnp.min(jnp.where(gate == mx, jl, n_fp), axis=1, keepdims=True)
        out = jnp.where(ol == r, idx, out)
        gate = jnp.where(jl == idx, -jnp.inf, gate)
    o_ref[...] = out


def _sample_gate(q3, kmean_l):
    db, _, a = q3.shape
    n_fp = kmean_l.shape[2]
    n_heads = a // HEAD_DIM
    return pl.pallas_call(
        _sgate_kernel,
        out_shape=jax.ShapeDtypeStruct((db, n_heads, LANES), jnp.int32),
        grid=(db,),
        in_specs=[pl.BlockSpec((None, 1, a), lambda b: (b, 0, 0)),
                  pl.BlockSpec((None, a, n_fp), lambda b: (b, 0, 0))],
        out_specs=pl.BlockSpec((None, n_heads, LANES), lambda b: (b, 0, 0)),
        compiler_params=_cparams(("arbitrary",)),
        name="sample_gate",
    )(q3, kmean_l)


def _sattn_copies(pt_ref, sel_ref, ck_ref, cv_ref, kb_ref, vb_ref, sem_ref, layer, slot, b, n_heads, ppb):
    out = []
    for h in range(n_heads):
        for r in range(MOBA_TOPK):
            blk = sel_ref[b, h * MOBA_TOPK + r]
            for pg in range(ppb):
                phys = pt_ref[b, blk * ppb + pg]
                i = r * ppb + pg
                out.append(pltpu.make_async_copy(ck_ref.at[layer, phys, h], kb_ref.at[slot, h, i], sem_ref.at[slot]))
                out.append(pltpu.make_async_copy(cv_ref.at[layer, phys, h], vb_ref.at[slot, h, i], sem_ref.at[slot]))
    return out


def _sattn_kernel(pt_ref, sel_ref, q_ref, k_ref, v_ref, ck_ref, cv_ref, o_ref, kb_ref, vb_ref, sem_ref, *, layer, past_len):
    b = pl.program_id(0)
    n_b = pl.num_programs(0)
    slot = b % 2
    n_heads, n_pg, hd, page = kb_ref.shape[1:]
    ppb = MOBA_BLOCK // page
    args = (pt_ref, sel_ref, ck_ref, cv_ref, kb_ref, vb_ref, sem_ref, layer)

    @pl.when(b == 0)
    def _():
        for c in _sattn_copies(*args, 0, b, n_heads, ppb):
            c.start()

    @pl.when(b + 1 < n_b)
    def _():
        for c in _sattn_copies(*args, 1 - slot, b + 1, n_heads, ppb):
            c.start()

    for c in _sattn_copies(*args, slot, b, n_heads, ppb):
        c.wait()

    n_sel = n_pg * page
    kpos = lax.broadcasted_iota(jnp.int32, (1, n_sel), 1)
    outs = []
    for h in range(n_heads):
        hs = slice(h * hd, (h + 1) * hd)
        slope = 2.0 ** (-8.0 * (h + 1.0) / n_heads)
        qf = (q_ref[:, hs] * HEAD_DIM ** -0.5).astype(BF16)
        q8 = jnp.broadcast_to(qf, (8, hd))
        l_sel = jnp.concatenate([jnp.dot(q8, kb_ref[slot, h, i].astype(BF16), preferred_element_type=F32)
                                 for i in range(n_pg)], axis=1)[0:1]
        blk_of = jnp.zeros((1, n_sel), jnp.int32)
        for r in range(MOBA_TOPK):
            blk_of = jnp.where(kpos // MOBA_BLOCK == r, sel_ref[b, h * MOBA_TOPK + r], blk_of)
        pos_sel = blk_of * MOBA_BLOCK + kpos % MOBA_BLOCK
        l_sel = l_sel - slope * (past_len - pos_sel).astype(F32)
        kown = k_ref[:, hs].astype(BF16).astype(F32)
        vown = v_ref[:, hs].astype(BF16).astype(F32)
        l_own = jnp.sum(qf.astype(F32) * kown, axis=1, keepdims=True)
        mx = jnp.maximum(jnp.max(l_sel, axis=1, keepdims=True), l_own)
        e_sel = jnp.exp(l_sel - mx)
        e_own = jnp.exp(l_own - mx)
        den = jnp.sum(e_sel, axis=1, keepdims=True) + e_own
        p8 = jnp.broadcast_to((e_sel / den).astype(BF16), (8, n_sel))
        p_own = (e_own / den).astype(BF16).astype(F32)
        o_h = p_own * vown
        for i in range(n_pg):
            o_h = o_h + lax.dot_general(p8[:, i * page:(i + 1) * page], vb_ref[slot, h, i].astype(BF16), _NT,
                                        preferred_element_type=F32)[0:1]
        outs.append(o_h)
    o_ref[...] = jnp.concatenate(outs, axis=1)


def _sample_attn(page_table, sel, q3, k3, v3, cache_kt, cache_vt, layer, past_len):
    db, _, a = q3.shape
    n_heads, hd, page = cache_kt.shape[2:]
    ppb = MOBA_BLOCK // page
    row = pl.BlockSpec((None, 1, a), lambda b, pt, sl: (b, 0, 0))
    anyspec = pl.BlockSpec(memory_space=pl.ANY)
    grid_spec = pltpu.PrefetchScalarGridSpec(
        num_scalar_prefetch=2,
        grid=(db,),
        in_specs=[row, row, row, anyspec, anyspec],
        out_specs=row,
        scratch_shapes=[pltpu.VMEM((2, n_heads, MOBA_TOPK * ppb, hd, page), F32),
                        pltpu.VMEM((2, n_heads, MOBA_TOPK * ppb, hd, page), F32),
                        pltpu.SemaphoreType.DMA((2,))])
    return pl.pallas_call(
        functools.partial(_sattn_kernel, layer=layer, past_len=past_len),
        out_shape=jax.ShapeDtypeStruct((db, 1, a), F32),
        grid_spec=grid_spec,
        compiler_params=_cparams(("arbitrary",)),
        name="sample_attn",
    )(page_table, sel, q3, k3, v3, cache_kt, cache_vt)


def _spost_kernel(x_ref, attn_ref, u_ref, st_ref, ga_ref, sh_ref, sc_ref, wp_ref, ps_ref, wo_ref, g_ref, wr_ref,
                  xo_ref, h2_ref, lg_ref, *, past_len):
    a = attn_ref.shape[-1]
    u = u_ref[...]
    st = st_ref[...]
    n_st = st.shape[1]
    gw = a // len(POOL_WINDOWS)
    srow = lax.broadcasted_iota(jnp.int32, (1, n_st, a), 1)
    lane = lax.broadcasted_iota(jnp.int32, (1, n_st, a), 2)
    win = jnp.zeros((1, n_st, a), jnp.int32)
    cnt = jnp.zeros((1, a), F32)
    lane2 = lax.broadcasted_iota(jnp.int32, (1, a), 1)
    for g, w in enumerate(POOL_WINDOWS):
        win = jnp.where(lane // gw == g, w, win)
        cnt = jnp.where(lane2 // gw == g, float(min(w, past_len + 1)), cnt)
    wsum = jnp.sum(jnp.where(srow >= n_st + 1 - win, st, 0.0), axis=1) + u
    diff = (wsum / cnt - u).astype(BF16)
    pools = [jnp.dot(diff[:, g * gw:(g + 1) * gw], wp_ref[g], preferred_element_type=F32)
             for g in range(len(POOL_WINDOWS))]
    pool = jnp.concatenate(pools, axis=1) * ps_ref[...]
    mo = (jnp.dot(attn_ref[...].astype(BF16), wo_ref[0:a, :], preferred_element_type=F32)
          + jnp.dot(pool.astype(BF16), wo_ref[a:, :], preferred_element_type=F32))
    xn = x_ref[...] + ga_ref[...] * mo
    xo_ref[...] = xn
    hb = (_rms(xn, g_ref[...]) * (1.0 + sc_ref[...]) + sh_ref[...]).astype(BF16)
    h2_ref[...] = hb
    lg_ref[...] = lax.dot_general(wr_ref[...], hb, _NT, preferred_element_type=F32)


def _sample_post(x, attn, u, state, ga1, sh2, sc2, w_pool_b, pool_scale, w_out_b, g_ffn, wr_b, past_len):
    db, d = x.shape
    ne = wr_b.shape[0]
    ins = (x, attn, u, state, ga1, sh2, sc2, w_pool_b, pool_scale, w_out_b, g_ffn, wr_b)
    full = lambda arr: pl.BlockSpec(arr.shape, lambda i, n=arr.ndim: (0,) * n)
    outs = (jax.ShapeDtypeStruct((db, d), F32), jax.ShapeDtypeStruct((db, d), BF16), jax.ShapeDtypeStruct((ne, db), F32))
    return pl.pallas_call(
        functools.partial(_spost_kernel, past_len=past_len),
        out_shape=outs,
        grid=(1,),
        in_specs=[full(x_) for x_ in ins],
        out_specs=tuple(full(o) for o in outs),
        compiler_params=_cparams(("arbitrary",)),
        name="sample_post",
    )(*ins)


def _smoe_kernel(h_ref, cb_ref, w1_ref, w3_ref, w2_ref, xn_ref, ga_ref, o_ref, acc_ref):
    e = pl.program_id(0)

    @pl.when(e == 0)
    def _():
        acc_ref[...] = jnp.zeros_like(acc_ref)

    h = h_ref[...]
    a = jnp.dot(h, w1_ref[...], preferred_element_type=F32)
    g = jnp.dot(h, w3_ref[...], preferred_element_type=F32)
    y = jnp.dot(((a * jax.nn.sigmoid(a)) * g).astype(BF16), w2_ref[...], preferred_element_type=F32)
    acc_ref[...] += y.astype(BF16).astype(F32) * cb_ref[...].astype(BF16).astype(F32)

    @pl.when(e == pl.num_programs(0) - 1)
    def _():
        o_ref[...] = xn_ref[...] + ga_ref[...] * acc_ref[...]


def _sample_moe(h2b, comb, w1_b, w3_b, w2_b, xn, ga2):
    db, d = h2b.shape
    n_e, _, de = w1_b.shape
    cst = lambda e: (0, 0)
    return pl.pallas_call(
        _smoe_kernel,
        out_shape=jax.ShapeDtypeStruct((db, d), F32),
        grid=(n_e,),
        in_specs=[pl.BlockSpec((db, d), cst),
                  pl.BlockSpec((None, db, 1), lambda e: (e, 0, 0)),
                  pl.BlockSpec((None, d, de), lambda e: (e, 0, 0)),
                  pl.BlockSpec((None, d, de), lambda e: (e, 0, 0)),
                  pl.BlockSpec((None, de, d), lambda e: (e, 0, 0)),
                  pl.BlockSpec((db, d), cst),
                  pl.BlockSpec((db, d), cst)],
        out_specs=pl.BlockSpec((db, d), cst),
        scratch_shapes=[pltpu.VMEM((db, d), F32)],
        compiler_params=_cparams(("arbitrary",)),
        name="sample_moe",
    )(h2b, comb, w1_b, w3_b, w2_b, xn, ga2)


def _sample_layer(x, mod_s, layer, kmean_l, cache_kt, cache_vt, page_table, state_l, g_mix_l, w_in_b_l, w_pool_b_l,
                  pool_scale_l, w_out_b_l, g_ffn_l, wr_b, router_bias, w1_b_l, w3_b_l, w2_b_l, past_len):
    db, d = x.shape
    a = kmean_l.shape[1]
    n_e = w1_b_l.shape[0]
    sh1, sc1, ga1, sh2, sc2, ga2 = [mod_s[:, i] for i in range(6)]
    r = _sample_pre(x, sh1, sc1, g_mix_l, w_in_b_l)
    q3, k3, v3, u = r[:, None, :a], r[:, None, a:2 * a], r[:, None, 2 * a:3 * a], r[:, 3 * a:]
    sel = _sample_gate(q3, kmean_l)[:, :, :MOBA_TOPK].reshape(db, -1)
    attn = _sample_attn(page_table, sel, q3, k3, v3, cache_kt, cache_vt, layer, past_len)[:, 0]
    xn, h2b, lgt = _sample_post(x, attn, u, state_l, ga1, sh2, sc2, w_pool_b_l, pool_scale_l, w_out_b_l,
                                g_ffn_l, wr_b, past_len)
    idx, _, gate, _ = _route(lgt[None], router_bias, db)
    comb = jnp.sum(jnp.where(idx[None, :, :] == jnp.arange(n_e)[:, None, None], gate[None], 0.0), axis=-1)
    x_out = _sample_moe(h2b, comb[..., None], w1_b_l, w3_b_l, w2_b_l, xn, ga2)
    return x_out, k3, v3, u


def kernel(x_prompt, x_sample, cache_k, cache_v, state_pool, page_table, c_prompt, c_sample,
           w_ada, b_ada, g_mix, w_in, w_pool, pool_scale, w_out, g_ffn, w_router, router_bias,
           w1, w3, w2, g_final):
    b, l, d = x_prompt.shape
    db, ds, _ = x_sample.shape
    depth = w_ada.shape[0]
    n_heads = cache_k.shape[3]
    a = n_heads * HEAD_DIM
    n_experts = w1.shape[1]
    tm = 512
    assert l % tm == 0 and tm % MOBA_BLOCK == 0 and a % LANES == 0 and n_experts <= 16

    mod = _ada(jnp.concatenate([c_prompt, c_sample], axis=0), w_ada, b_ada)
    mod = mod.reshape(depth, b + db, 6, d)
    slopes = _alibi_slopes(n_heads)

    w_q, w_k, w_v = w_in[:, :, :a], w_in[:, :, a:2 * a], w_in[:, :, 2 * a:3 * a]

    def head_pad(w):
        w = w.reshape(depth, d, n_heads, HEAD_DIM)
        return jnp.pad(w, ((0, 0), (0, 0), (0, 0), (0, HEAD_PAD - HEAD_DIM))).reshape(depth, d, n_heads * HEAD_PAD)

    wn_b = w_in[:, :, a:].astype(BF16)
    wa_b = jnp.concatenate([head_pad(w_q), head_pad(w_k)], axis=-1).astype(BF16)
    wvt_b = jnp.swapaxes(w_v, 1, 2).astype(BF16)
    slopes_np = 2.0 ** (-8.0 * (np.arange(n_heads, dtype=np.float64) + 1.0) / n_heads)
    key_term = (slopes_np[:, None] * np.arange(MOBA_BLOCK)[None, :]).astype(np.float32)
    assert np.array_equal(key_term.astype(BF16).astype(np.float32), key_term), \
        "ALiBi key term must be exact in bfloat16 to ride in the padded key lanes"
    alibi_lane = np.zeros((n_heads, HEAD_PAD), np.float32)
    alibi_lane[:, HEAD_DIM] = 1.0
    orow = jnp.asarray(alibi_lane.reshape(1, -1))
    srow = jnp.asarray((alibi_lane * slopes_np[:, None].astype(np.float32)).reshape(1, -1))
    w_out_b = w_out.astype(BF16)
    w_pool_b = w_pool.astype(BF16)
    w1_b, w3_b, w2_b = w1.astype(BF16), w3.astype(BF16), w2.astype(BF16)
    wr_b = w_router.T.astype(BF16)

    w_in_b = w_in.astype(BF16)
    page = cache_k.shape[2]
    n_pool = cache_k.shape[1]
    past_len = page_table.shape[1] * page
    n_fp = past_len // MOBA_BLOCK
    assert ds == 1 and past_len % MOBA_BLOCK == 0 and n_fp >= MOBA_TOPK and MOBA_BLOCK % page == 0
    cache_kt = jnp.transpose(cache_k, (0, 1, 3, 4, 2))
    cache_vt = jnp.transpose(cache_v, (0, 1, 3, 4, 2))
    kmean_s = _kmean_sample(cache_kt.reshape(depth, n_pool, a, page), page_table, n_fp)

    xp, xs = x_prompt, x_sample.reshape(db, d)
    kp, vp, up, kq, vq, uq = [], [], [], [], [], []
    for li in range(depth):
        mod_p = mod[li, :b]
        k, v, u, qa, ka, vt, kmean = _qkvu(xp, mod_p, g_mix[li][None], wn_b[li], wa_b[li], wvt_b[li], srow, orow, tm)
        attn = _moba_prompt(slopes, qa, ka, vt, kmean)
        xn, h2, lg = _mix(xp, attn, u, mod_p, w_pool_b[li], pool_scale[li][None], w_out_b[li],
                          g_ffn[li][None], wr_b, tm)
        yb, dest2, gate = _moe(h2.reshape(b * l, d), lg, router_bias, w1_b[li], w3_b[li], w2_b[li], tm)
        xp = _combine(dest2, yb, xn, mod_p, gate, g_final[None], MOE_TOK, li == depth - 1)
        kp.append(k.reshape(b, l, n_heads, HEAD_DIM))
        vp.append(v.reshape(b, l, n_heads, HEAD_DIM))
        up.append(u[:, l - POOL_STATE_LEN:])

        xs, k3, v3, u_s = _sample_layer(
            xs, mod[li, b:], li, kmean_s[li], cache_kt, cache_vt, page_table, state_pool[li], g_mix[li][None],
            w_in_b[li], w_pool_b[li], pool_scale[li][None], w_out_b[li], g_ffn[li][None], wr_b, router_bias,
            w1_b[li], w3_b[li], w2_b[li], past_len)
        kq.append(k3.reshape(db, ds, n_heads, HEAD_DIM))
        vq.append(v3.reshape(db, ds, n_heads, HEAD_DIM))
        uq.append(jnp.concatenate([state_pool[li][:, ds:], u_s[:, None]], axis=1))

    y_prompt = xp
    y_sample = _final_norm(xs, g_final[None]).reshape(db, ds, d)
    return (y_prompt, y_sample, jnp.stack(kp), jnp.stack(vp), jnp.stack(up),
            jnp.stack(kq), jnp.stack(vq), jnp.stack(uq))
```

```python
import functools

import numpy as np
import jax
import jax.numpy as jnp
from jax import lax
from jax.experimental import pallas as pl
from jax.experimental.pallas import tpu as pltpu

F32 = jnp.float32
BF16 = jnp.bfloat16

HEAD_DIM = 64
MOBA_BLOCK = 256
MOBA_TOPK = 3
POOL_WINDOWS = (2, 4, 8, 16)
POOL_STATE_LEN = max(POOL_WINDOWS) - 1
POOL_HALO = 16
N_EXPERT_GROUPS = 4
TOP_K = 2
NORM_EPS = 1e-6
LANES = 128
HEAD_PAD = 128
ATTN_GROUP = 4
ATTN_HEADS = 4
KMEAN_PAGES = 16
MOE_TOK = 256
MOE_ROWS = 256
VMEM_LIMIT = 48 * 1024 * 1024

_NT = (((1,), (1,)), ((), ()))
_TN = (((0,), (0,)), ((), ()))


def _cparams(sem):
    return pltpu.CompilerParams(dimension_semantics=sem, vmem_limit_bytes=VMEM_LIMIT)


def _rms(x, g):
    return x * lax.rsqrt(jnp.mean(x * x, axis=-1, keepdims=True) + NORM_EPS) * g


def _ada_kernel(c_ref, w_ref, b_ref, o_ref):
    c = c_ref[...]
    s = c * jax.nn.sigmoid(c)
    o_ref[...] = jnp.dot(s.astype(BF16), w_ref[...].astype(BF16), preferred_element_type=F32) + b_ref[...]


def _ada(c, w_ada, b_ada):
    depth, d, n = w_ada.shape
    nb = c.shape[0]
    tn = 1536
    return pl.pallas_call(
        _ada_kernel,
        out_shape=jax.ShapeDtypeStruct((depth, nb, n), F32),
        grid=(depth, n // tn),
        in_specs=[pl.BlockSpec((nb, d), lambda l, j: (0, 0)),
                  pl.BlockSpec((None, d, tn), lambda l, j: (l, 0, j)),
                  pl.BlockSpec((None, 1, tn), lambda l, j: (l, 0, j))],
        out_specs=pl.BlockSpec((None, nb, tn), lambda l, j: (l, 0, j)),
        compiler_params=_cparams(("arbitrary", "arbitrary")),
        name="ada",
    )(c, w_ada, b_ada.reshape(depth, 1, n))


def _qkvu_kernel(x_ref, mod_ref, g_ref, wn_ref, wa_ref, wvt_ref, srow_ref, orow_ref,
                 k_ref, v_ref, u_ref, qa_ref, ka_ref, vt_ref, km_ref):
    a = k_ref.shape[-1]
    tm = x_ref.shape[0]
    ap = qa_ref.shape[-1]
    h = _rms(x_ref[...], g_ref[...]) * (1.0 + mod_ref[1:2, :]) + mod_ref[0:1, :]
    hb = h.astype(BF16)
    rn = jnp.dot(hb, wn_ref[...], preferred_element_type=F32)
    k_ref[...] = rn[:, :a]
    v_ref[...] = rn[:, a:2 * a]
    u_ref[...] = rn[:, 2 * a:]
    ra = jnp.dot(hb, wa_ref[...], preferred_element_type=F32)
    qa_ref[...] = (ra[:, :ap] * HEAD_DIM ** -0.5 + orow_ref[...]).astype(BF16)
    kp = ra[:, ap:]
    ridx = (lax.broadcasted_iota(jnp.int32, (tm, 1), 0) % MOBA_BLOCK).astype(F32)
    ka_ref[...] = (kp + ridx * srow_ref[...]).astype(BF16)
    vt = lax.dot_general(wvt_ref[...], hb, _NT, preferred_element_type=F32)
    for s in range(km_ref.shape[0]):
        rows = slice(s * MOBA_BLOCK, (s + 1) * MOBA_BLOCK)
        km_ref[s:s + 1, :] = jnp.mean(kp[rows], axis=0, keepdims=True)
        vt_ref[s] = vt[:, rows].astype(BF16)


def _qkvu(x, mod, g, wn_b, wa_b, wvt_b, srow, orow, tm):
    b, l, d = x.shape
    a = wn_b.shape[1] // 3
    ap = wa_b.shape[1] // 2
    nkb_t = tm // MOBA_BLOCK
    tok = lambda bi, t: (bi, t, 0)
    cst = lambda bi, t: (0, 0)
    outs = pl.pallas_call(
        _qkvu_kernel,
        out_shape=(jax.ShapeDtypeStruct((b, l, a), F32),
                   jax.ShapeDtypeStruct((b, l, a), F32),
                   jax.ShapeDtypeStruct((b, l, a), F32),
                   jax.ShapeDtypeStruct((b, l, ap), BF16),
                   jax.ShapeDtypeStruct((b, l, ap), BF16),
                   jax.ShapeDtypeStruct((b, l // MOBA_BLOCK, a, MOBA_BLOCK), BF16),
                   jax.ShapeDtypeStruct((b, l // tm, nkb_t, ap), F32)),
        grid=(b, l // tm),
        in_specs=[pl.BlockSpec((None, tm, d), tok),
                  pl.BlockSpec((None, 6, d), lambda bi, t: (bi, 0, 0)),
                  pl.BlockSpec((1, d), cst),
                  pl.BlockSpec((d, 3 * a), cst),
                  pl.BlockSpec((d, 2 * ap), cst),
                  pl.BlockSpec((a, d), cst),
                  pl.BlockSpec((1, ap), cst),
                  pl.BlockSpec((1, ap), cst)],
        out_specs=(pl.BlockSpec((None, tm, a), tok),) * 3 + (pl.BlockSpec((None, tm, ap), tok),) * 2
        + (pl.BlockSpec((None, nkb_t, a, MOBA_BLOCK), lambda bi, t: (bi, t, 0, 0)),
           pl.BlockSpec((None, None, nkb_t, ap), lambda bi, t: (bi, t, 0, 0))),
        compiler_params=_cparams(("arbitrary", "arbitrary")),
        name="qkvu",
    )(x, mod, g, wn_b, wa_b, wvt_b, srow, orow)
    k, v, u, qa, ka, vt, km = outs
    return k, v, u, qa, ka, vt, km.reshape(b, l // MOBA_BLOCK, ap)


def _attn_kernel(sl_ref, q_ref, k_ref, vt_ref, km_ref, o_ref, seln_ref):
    p = pl.program_id(1)
    i = pl.program_id(2)
    blk = MOBA_BLOCK
    n_kb = km_ref.shape[0]
    nh = ATTN_HEADS
    kidx = lax.broadcasted_iota(jnp.int32, (blk, blk), 0)
    qidx = lax.broadcasted_iota(jnp.int32, (blk, blk), 1)
    causal = kidx <= qidx
    jrow = lax.broadcasted_iota(jnp.int32, (n_kb, blk), 0)
    start = pl.multiple_of(i * blk, blk)
    hls = [slice(s * HEAD_PAD, (s + 1) * HEAD_PAD) for s in range(nh)]
    hrs = [slice(s * HEAD_DIM, (s + 1) * HEAD_DIM) for s in range(nh)]
    slopes = [sl_ref[nh * p + s] for s in range(nh)]
    init = []
    for s in range(nh):
        qh = q_ref[:, hls[s]]
        gate = lax.dot_general(km_ref[:, hls[s]].astype(BF16), qh, _NT, preferred_element_type=F32)
        gate = jnp.where(jrow < i, gate, -jnp.inf)
        sel = jnp.zeros(gate.shape, jnp.bool_)
        for r in range(MOBA_TOPK):
            mx = jnp.max(gate, axis=0, keepdims=True)
            idx = jnp.min(jnp.where(gate == mx, jrow, n_kb), axis=0, keepdims=True)
            pick = jrow == idx
            sel = sel | (pick & (i > r))
            gate = jnp.where(pick, -jnp.inf, gate)
        seln_ref[s] = jnp.where(sel, 0.0, -jnp.inf)
        st = lax.dot_general(k_ref[pl.ds(start, blk), hls[s]], qh, _NT, preferred_element_type=F32)
        st = jnp.where(causal, st, -jnp.inf)
        m0 = jnp.max(st, axis=0, keepdims=True)
        pt = jnp.exp(st - m0)
        l0 = jnp.sum(pt, axis=0, keepdims=True)
        acc0 = jnp.dot(vt_ref[i, hrs[s], :], pt.astype(BF16), preferred_element_type=F32)
        init.append((m0, l0, acc0))

    def scores(g, s, m_old):
        qh = q_ref[:, hls[s]]
        sts, offs = [], []
        m_new = m_old
        for u in range(ATTN_GROUP):
            j = g * ATTN_GROUP + u
            ks = pl.multiple_of(j * blk, blk)
            st_u = lax.dot_general(k_ref[pl.ds(ks, blk), hls[s]], qh, _NT, preferred_element_type=F32)
            cj = slopes[s] * ((j - i) * blk).astype(F32)
            off = cj + seln_ref[s, pl.ds(j, 1), :]
            m_new = jnp.maximum(m_new, jnp.max(st_u, axis=0, keepdims=True) + off)
            sts.append(st_u)
            offs.append(off)
        return sts, offs, m_new

    def accumulate(g, s, state, staged):
        m_old, l_old, acc_old = state
        sts, offs, m_new = staged
        alpha = jnp.exp(m_old - m_new)
        l_new = alpha * l_old
        acc_new = alpha * acc_old
        for u in range(ATTN_GROUP):
            j = g * ATTN_GROUP + u
            pt_u = jnp.exp(sts[u] - (m_new - offs[u]))
            l_new = l_new + jnp.sum(pt_u, axis=0, keepdims=True)
            acc_new = acc_new + jnp.dot(vt_ref[j, hrs[s], :], pt_u.astype(BF16), preferred_element_type=F32)
        return m_new, l_new, acc_new

    def group(g, carry):
        staged = scores(g, 0, carry[0][0])
        out = []
        for s in range(nh):
            nxt = scores(g, s + 1, carry[s + 1][0]) if s + 1 < nh else None
            out.append(accumulate(g, s, carry[s], staged))
            staged = nxt
        return tuple(out)

    final = lax.fori_loop(0, (i + ATTN_GROUP - 1) // ATTN_GROUP, group, tuple(init))
    ot = jnp.concatenate([acc_f / l_f for (_, l_f, acc_f) in final], axis=0)
    o_ref[...] = ot.T.astype(o_ref.dtype)


def _moba_prompt(slopes, qa, ka, vt, kmean):
    b, l, ap = qa.shape
    n_kb = l // MOBA_BLOCK
    blk = MOBA_BLOCK
    nh = ATTN_HEADS
    n_hg = ap // (nh * HEAD_PAD)
    assert n_kb % ATTN_GROUP == 0 and ap % (nh * HEAD_PAD) == 0
    grid_spec = pltpu.PrefetchScalarGridSpec(
        num_scalar_prefetch=1,
        grid=(b, n_hg, n_kb),
        in_specs=[pl.BlockSpec((None, blk, nh * HEAD_PAD), lambda bi, p, i, sl: (bi, i, p)),
                  pl.BlockSpec((None, l, nh * HEAD_PAD), lambda bi, p, i, sl: (bi, 0, p)),
                  pl.BlockSpec((None, n_kb, nh * HEAD_DIM, blk), lambda bi, p, i, sl: (bi, 0, p, 0)),
                  pl.BlockSpec((None, n_kb, nh * HEAD_PAD), lambda bi, p, i, sl: (bi, 0, p))],
        out_specs=pl.BlockSpec((None, blk, nh * HEAD_DIM), lambda bi, p, i, sl: (bi, i, p)),
        scratch_shapes=[pltpu.VMEM((nh, n_kb, blk), F32)])
    return pl.pallas_call(
        _attn_kernel,
        out_shape=jax.ShapeDtypeStruct((b, l, n_hg * nh * HEAD_DIM), BF16),
        grid_spec=grid_spec,
        compiler_params=_cparams(("arbitrary", "arbitrary", "arbitrary")),
        name="moba_prompt",
    )(slopes, qa, ka, vt, kmean)


def _mix_kernel(x_ref, attn_ref, u_ref, halo_ref, mod_ref, wp_ref, ps_ref, wo_ref, g_ref, wr_ref,
                xo_ref, h2_ref, lg_ref, ext_ref):
    t = pl.program_id(1)
    tm = x_ref.shape[0]
    a = attn_ref.shape[-1]
    halo = jnp.where(t == 0, 0.0, halo_ref[...])
    ext_ref[0:POOL_HALO, :] = halo
    ext_ref[POOL_HALO:POOL_HALO + tm, :] = u_ref[...]
    pos = t * tm + lax.broadcasted_iota(jnp.int32, (tm, 1), 0)
    gw = a // len(POOL_WINDOWS)
    pools = []
    for g, w in enumerate(POOL_WINDOWS):
        ln = slice(g * gw, (g + 1) * gw)
        acc = ext_ref[POOL_HALO:POOL_HALO + tm, ln]
        for s in range(1, w):
            acc = acc + ext_ref[POOL_HALO - s:POOL_HALO - s + tm, ln]
        cnt = jnp.minimum(w, pos + 1).astype(F32)
        diff = acc / cnt - u_ref[:, ln]
        pools.append(jnp.dot(diff.astype(BF16), wp_ref[g], preferred_element_type=F32))
    pool = jnp.concatenate(pools, axis=1) * ps_ref[...]
    mo = (jnp.dot(attn_ref[...], wo_ref[0:a, :], preferred_element_type=F32)
          + jnp.dot(pool.astype(BF16), wo_ref[a:, :], preferred_element_type=F32))
    xn = x_ref[...] + mod_ref[2:3, :] * mo
    xo_ref[...] = xn
    h2 = _rms(xn, g_ref[...]) * (1.0 + mod_ref[4:5, :]) + mod_ref[3:4, :]
    h2_ref[...] = h2
    lg_ref[...] = lax.dot_general(wr_ref[...], h2.astype(BF16), _NT, preferred_element_type=F32)


def _mix(x, attn, u, mod, w_pool_b, pool_scale, w_out_b, g_ffn, wr_b, tm):
    b, l, d = x.shape
    a = attn.shape[-1]
    ne = wr_b.shape[0]
    tok = lambda bi, t: (bi, t, 0)
    cst2 = lambda bi, t: (0, 0)
    hb = tm // POOL_HALO
    return pl.pallas_call(
        _mix_kernel,
        out_shape=(jax.ShapeDtypeStruct((b, l, d), F32),
                   jax.ShapeDtypeStruct((b, l, d), F32),
                   jax.ShapeDtypeStruct((b, ne, l), F32)),
        grid=(b, l // tm),
        in_specs=[pl.BlockSpec((None, tm, d), tok),
                  pl.BlockSpec((None, tm, a), tok),
                  pl.BlockSpec((None, tm, a), tok),
                  pl.BlockSpec((None, POOL_HALO, a), lambda bi, t: (bi, jnp.maximum(t * hb - 1, 0), 0)),
                  pl.BlockSpec((None, 6, d), lambda bi, t: (bi, 0, 0)),
                  pl.BlockSpec(w_pool_b.shape, lambda bi, t: (0, 0, 0)),
                  pl.BlockSpec((1, a), cst2),
                  pl.BlockSpec((d, d), cst2),
                  pl.BlockSpec((1, d), cst2),
                  pl.BlockSpec((ne, d), cst2)],
        out_specs=(pl.BlockSpec((None, tm, d), tok),
                   pl.BlockSpec((None, tm, d), tok),
                   pl.BlockSpec((None, ne, tm), lambda bi, t: (bi, 0, t))),
        scratch_shapes=[pltpu.VMEM((tm + POOL_HALO, a), F32)],
        compiler_params=_cparams(("arbitrary", "arbitrary")),
        name="mix",
    )(x, attn, u, u, mod, w_pool_b, pool_scale, w_out_b, g_ffn, wr_b)


def _moe_kernel(be_ref, na_ref, x_ref, w1_ref, w3_ref, w2_ref, o_ref):
    i = pl.program_id(0)

    @pl.when(i < na_ref[0])
    def _():
        x = x_ref[...].astype(BF16)
        a = jnp.dot(x, w1_ref[...], preferred_element_type=F32)
        g = jnp.dot(x, w3_ref[...], preferred_element_type=F32)
        hmid = (a * jax.nn.sigmoid(a)) * g
        o_ref[...] = jnp.dot(hmid.astype(BF16), w2_ref[...], preferred_element_type=F32)

    @pl.when(i >= na_ref[0])
    def _():
        o_ref[...] = jnp.zeros_like(o_ref)


def _moe_blocks(block_e, n_active, xb, w1_b, w3_b, w2_b):
    cap, d = xb.shape
    de = w1_b.shape[-1]
    n_blocks = cap // MOE_ROWS
    grid_spec = pltpu.PrefetchScalarGridSpec(
        num_scalar_prefetch=2,
        grid=(n_blocks,),
        in_specs=[pl.BlockSpec((MOE_ROWS, d), lambda i, be, na: (i, 0)),
                  pl.BlockSpec((None, d, de), lambda i, be, na: (be[i], 0, 0)),
                  pl.BlockSpec((None, d, de), lambda i, be, na: (be[i], 0, 0)),
                  pl.BlockSpec((None, de, d), lambda i, be, na: (be[i], 0, 0))],
        out_specs=pl.BlockSpec((MOE_ROWS, d), lambda i, be, na: (i, 0)))
    return pl.pallas_call(
        _moe_kernel,
        out_shape=jax.ShapeDtypeStruct((cap, d), F32),
        grid_spec=grid_spec,
        compiler_params=_cparams(("arbitrary",)),
        name="moe_blocks",
    )(block_e, n_active, xb, w1_b, w3_b, w2_b)


def _route_rows(lg, bias):
    n_e = lg.shape[0]
    epg = n_e // N_EXPERT_GROUPS
    sc = jax.nn.sigmoid(lg)
    bi = sc + bias
    s_rows = [sc[e:e + 1, :] for e in range(n_e)]
    b_rows = [bi[e:e + 1, :] for e in range(n_e)]
    gs = []
    for g in range(N_EXPERT_GROUPS):
        v = b_rows[g * epg:(g + 1) * epg]
        best = None
        for a_ in range(epg):
            for b_ in range(a_ + 1, epg):
                pr = v[a_] + v[b_]
                best = pr if best is None else jnp.maximum(best, pr)
        gs.append(best)
    grp = jnp.zeros(gs[0].shape, jnp.int32)
    gbest = gs[0]
    for g in range(1, N_EXPERT_GROUPS):
        up = gs[g] > gbest
        gbest = jnp.where(up, gs[g], gbest)
        grp = jnp.where(up, g, grp)
    c = [b_rows[j] for j in range(epg)]
    d = [s_rows[j] for j in range(epg)]
    for g in range(1, N_EXPERT_GROUPS):
        hit = grp == g
        c = [jnp.where(hit, b_rows[g * epg + j], c[j]) for j in range(epg)]
        d = [jnp.where(hit, s_rows[g * epg + j], d[j]) for j in range(epg)]

    def first_argmax(vals):
        idx = jnp.zeros(vals[0].shape, jnp.int32)
        best = vals[0]
        for j in range(1, len(vals)):
            up = vals[j] > best
            best = jnp.where(up, vals[j], best)
            idx = jnp.where(up, j, idx)
        return idx

    i1 = first_argmax(c)
    i2 = first_argmax([jnp.where(i1 == j, -jnp.inf, c[j]) for j in range(epg)])
    g1 = d[0]
    g2 = d[0]
    for j in range(1, epg):
        g1 = jnp.where(i1 == j, d[j], g1)
        g2 = jnp.where(i2 == j, d[j], g2)
    tot = g1 + g2
    return grp * epg + i1, grp * epg + i2, g1 / tot, g2 / tot


def _route_kernel(lg_ref, bias_ref, tri_ref, io_ref, fo_ref, cnt_ref, run_ref):
    first = (pl.program_id(0) == 0) & (pl.program_id(1) == 0)

    @pl.when(first)
    def _():
        run_ref[...] = jnp.zeros_like(run_ref)

    lg = lg_ref[...]
    n_e, tr = lg.shape
    e0, e1, g0, g1 = _route_rows(lg, bias_ref[...])
    erow = lax.broadcasted_iota(jnp.int32, (n_e, tr), 0)
    oh0 = erow == e0
    oh1 = erow == e1
    both = (oh0 | oh1).astype(F32)
    before = jnp.dot(both.astype(BF16), tri_ref[...], preferred_element_type=F32) + run_ref[...]
    r0 = jnp.sum(jnp.where(oh0, before, 0.0), axis=0, keepdims=True)
    r1 = jnp.sum(jnp.where(oh1, before, 0.0), axis=0, keepdims=True)
    run_ref[...] = run_ref[...] + jnp.sum(both, axis=1, keepdims=True)
    io_ref[...] = jnp.concatenate([e0, e1, r0.astype(jnp.int32), r1.astype(jnp.int32),
                                   jnp.zeros((4, tr), jnp.int32)], axis=0)
    fo_ref[...] = jnp.concatenate([g0, g1, jnp.zeros((6, tr), F32)], axis=0)
    cnt_ref[...] = jnp.broadcast_to(run_ref[...], cnt_ref.shape)


def _route(lgt, router_bias, tr):
    b, n_e, l = lgt.shape
    tri = (jnp.arange(tr)[:, None] < jnp.arange(tr)[None, :]).astype(BF16)
    io, fo, cnt = pl.pallas_call(
        _route_kernel,
        out_shape=(jax.ShapeDtypeStruct((b, 8, l), jnp.int32),
                   jax.ShapeDtypeStruct((b, 8, l), F32),
                   jax.ShapeDtypeStruct((n_e, LANES), F32)),
        grid=(b, l // tr),
        in_specs=[pl.BlockSpec((None, n_e, tr), lambda bi, t: (bi, 0, t)),
                  pl.BlockSpec((n_e, 1), lambda bi, t: (0, 0)),
                  pl.BlockSpec((tr, tr), lambda bi, t: (0, 0))],
        out_specs=(pl.BlockSpec((None, 8, tr), lambda bi, t: (bi, 0, t)),
                   pl.BlockSpec((None, 8, tr), lambda bi, t: (bi, 0, t)),
                   pl.BlockSpec((n_e, LANES), lambda bi, t: (0, 0))),
        scratch_shapes=[pltpu.VMEM((n_e, 1), F32)],
        compiler_params=_cparams(("arbitrary", "arbitrary")),
        name="route",
    )(lgt, router_bias.astype(F32).reshape(n_e, 1), tri)
    idx = jnp.stack([io[:, 0], io[:, 1]], axis=-1).reshape(b * l, TOP_K)
    rank = jnp.stack([io[:, 2], io[:, 3]], axis=-1).reshape(b * l, TOP_K)
    gate = jnp.stack([fo[:, 0], fo[:, 1]], axis=-1).reshape(b * l, TOP_K)
    return idx, rank, gate, cnt[:, 0].astype(jnp.int32)


def _row_copy(src_ref, src_row, dst_ref, dst_row, sem_ref):
    return pltpu.make_async_copy(src_ref.at[pl.ds(src_row, 1)], dst_ref.at[pl.ds(dst_row, 1)], sem_ref.at[0])


def _dispatch_kernel(dest_ref, h_ref, xz_ref, xb_ref, sem_ref):
    del xz_ref
    td = dest_ref.shape[1] // TOP_K

    def issue(r, c):
        for k in range(TOP_K):
            _row_copy(h_ref, r, xb_ref, dest_ref[0, TOP_K * r + k], sem_ref).start(priority=k % 2)
        return c

    lax.fori_loop(0, td, issue, 0, unroll=8)

    def drain(r, c):
        _row_copy(h_ref, 0, xb_ref, 0, sem_ref).wait()
        return c

    lax.fori_loop(0, TOP_K * td, drain, 0, unroll=8)


def _dispatch(dest2, h2, cap):
    t, d = h2.shape
    n_steps = dest2.shape[0]
    anyspec = pl.BlockSpec(memory_space=pl.ANY)
    return pl.pallas_call(
        _dispatch_kernel,
        out_shape=jax.ShapeDtypeStruct((cap, d), h2.dtype),
        grid=(n_steps,),
        in_specs=[pl.BlockSpec((None, 1, dest2.shape[2]), lambda i: (i, 0, 0), memory_space=pltpu.SMEM),
                  pl.BlockSpec((dest2.shape[2] // TOP_K, d), lambda i: (i, 0)), anyspec],
        out_specs=anyspec,
        scratch_shapes=[pltpu.SemaphoreType.DMA((1,))],
        input_output_aliases={2: 0},
        compiler_params=_cparams(("arbitrary",)),
        name="moe_dispatch",
    )(dest2, h2, jnp.zeros((cap, d), h2.dtype))


def _combine_kernel(dest_ref, yb_ref, xn_ref, mod_ref, gate_ref, g_ref, o_ref, buf_ref, sem_ref, *, final):
    tc = xn_ref.shape[0]

    def issue(r, c):
        for k in range(TOP_K):
            pltpu.make_async_copy(yb_ref.at[pl.ds(dest_ref[0, TOP_K * r + k], 1)], buf_ref.at[k, pl.ds(r, 1)],
                                  sem_ref.at[0]).start(priority=k % 2)
        return c

    lax.fori_loop(0, tc, issue, 0, unroll=8)

    def drain(r, c):
        pltpu.make_async_copy(yb_ref.at[pl.ds(0, 1)], buf_ref.at[0, pl.ds(0, 1)], sem_ref.at[0]).wait()
        return c

    lax.fori_loop(0, TOP_K * tc, drain, 0, unroll=8)
    gate = gate_ref[...].astype(BF16).astype(F32)
    y = buf_ref[0].astype(BF16).astype(F32) * gate[:, 0:1]
    for k in range(1, TOP_K):
        y = y + buf_ref[k].astype(BF16).astype(F32) * gate[:, k:k + 1]
    x = xn_ref[...] + mod_ref[5:6, :] * y
    o_ref[...] = _rms(x, g_ref[...]) if final else x


def _combine(dest2, yb, xn, mod, gate, g_final, tc, final):
    b, l, d = xn.shape
    nt = l // tc
    anyspec = pl.BlockSpec(memory_space=pl.ANY)
    return pl.pallas_call(
        functools.partial(_combine_kernel, final=final),
        out_shape=jax.ShapeDtypeStruct((b, l, d), F32),
        grid=(b, nt),
        in_specs=[pl.BlockSpec((None, 1, TOP_K * tc), lambda bi, t: (bi * nt + t, 0, 0), memory_space=pltpu.SMEM),
                  anyspec,
                  pl.BlockSpec((None, tc, d), lambda bi, t: (bi, t, 0)),
                  pl.BlockSpec((None, 6, d), lambda bi, t: (bi, 0, 0)),
                  pl.BlockSpec((None, tc, TOP_K), lambda bi, t: (bi, t, 0)),
                  pl.BlockSpec((1, d), lambda bi, t: (0, 0))],
        out_specs=pl.BlockSpec((None, tc, d), lambda bi, t: (bi, t, 0)),
        scratch_shapes=[pltpu.VMEM((TOP_K, tc, d), F32), pltpu.SemaphoreType.DMA((1,))],
        compiler_params=_cparams(("arbitrary", "arbitrary")),
        name="moe_combine",
    )(dest2, yb, xn, mod, gate.reshape(b, l, TOP_K), g_final)


def _moe(h2, lgt, router_bias, w1_b, w3_b, w2_b, tr):
    t, d = h2.shape
    n_experts = w1_b.shape[0]
    idx, rank, gate, counts = _route(lgt, router_bias, tr)
    s = t * TOP_K
    padded = (counts + MOE_ROWS - 1) // MOE_ROWS * MOE_ROWS
    pad_end = jnp.cumsum(padded)
    pad_start = pad_end - padded
    onehot = idx[..., None] == jnp.arange(n_experts)
    dest = rank + jnp.sum(jnp.where(onehot, pad_start, 0), axis=-1)
    n_blocks = -(-s // MOE_ROWS) + n_experts
    cap = n_blocks * MOE_ROWS
    dest2 = dest.reshape(t // MOE_TOK, 1, MOE_TOK * TOP_K).astype(jnp.int32)
    xb = _dispatch(dest2, h2, cap)
    block_start = jnp.arange(n_blocks, dtype=pad_end.dtype) * MOE_ROWS
    block_e = jnp.minimum(jnp.sum(pad_end[None, :] <= block_start[:, None], axis=1), n_experts - 1).astype(jnp.int32)
    n_active = (pad_end[-1] // MOE_ROWS).astype(jnp.int32).reshape(1)
    yb = _moe_blocks(block_e, n_active, xb, w1_b, w3_b, w2_b)
    return yb, dest2, gate


def _final_kernel(x_ref, g_ref, o_ref):
    o_ref[...] = _rms(x_ref[...], g_ref[...])


def _final_norm(x2d, g):
    t, d = x2d.shape
    tm = min(t, 512)
    return pl.pallas_call(
        _final_kernel,
        out_shape=jax.ShapeDtypeStruct((t, d), F32),
        grid=(t // tm,),
        in_specs=[pl.BlockSpec((tm, d), lambda i: (i, 0)), pl.BlockSpec((1, d), lambda i: (0, 0))],
        out_specs=pl.BlockSpec((tm, d), lambda i: (i, 0)),
        compiler_params=_cparams(("arbitrary",)),
        name="final_norm",
    )(x2d, g)


def _alibi_slopes(n_heads):
    return 2.0 ** (-8.0 * (jnp.arange(n_heads, dtype=F32) + 1.0) / n_heads)


def _kmean_copy(cache_ref, buf_ref, sem_ref, pt_ref, slot, l, b, t, pg):
    phys = pt_ref[b, t * KMEAN_PAGES + pg]
    return pltpu.make_async_copy(cache_ref.at[l, phys], buf_ref.at[slot, pg], sem_ref.at[slot])


def _kmean_kernel(pt_ref, cache_ref, o_ref, buf_ref, sem_ref):
    l, b, t = pl.program_id(0), pl.program_id(1), pl.program_id(2)
    n_b, n_t = pl.num_programs(1), pl.num_programs(2)
    step = (l * n_b + b) * n_t + t
    slot = step % 2
    last = step == pl.num_programs(0) * n_b * n_t - 1

    @pl.when(step == 0)
    def _():
        for pg in range(KMEAN_PAGES):
            _kmean_copy(cache_ref, buf_ref, sem_ref, pt_ref, 0, l, b, t, pg).start()

    @pl.when(jnp.logical_not(last))
    def _():
        wrap_t = t + 1 == n_t
        wrap_b = wrap_t & (b + 1 == n_b)
        t2 = jnp.where(wrap_t, 0, t + 1)
        b2 = jnp.where(wrap_b, 0, jnp.where(wrap_t, b + 1, b))
        l2 = jnp.where(wrap_b, l + 1, l)
        for pg in range(KMEAN_PAGES):
            _kmean_copy(cache_ref, buf_ref, sem_ref, pt_ref, 1 - slot, l2, b2, t2, pg).start()

    for pg in range(KMEAN_PAGES):
        _kmean_copy(cache_ref, buf_ref, sem_ref, pt_ref, slot, l, b, t, pg).wait()

    @pl.when(t == 0)
    def _():
        o_ref[...] = jnp.zeros_like(o_ref)

    ppb = MOBA_BLOCK // buf_ref.shape[3]
    gblk = KMEAN_PAGES // ppb
    lane = lax.broadcasted_iota(jnp.int32, o_ref.shape, 1)
    acc = o_ref[...]
    for g in range(gblk):
        tot = buf_ref[slot, g * ppb]
        for q in range(1, ppb):
            tot = tot + buf_ref[slot, g * ppb + q]
        col = jnp.sum(tot, axis=1, keepdims=True) * (1.0 / MOBA_BLOCK)
        acc = jnp.where(lane == t * gblk + g, col, acc)
    o_ref[...] = acc


def _kmean_sample(cache_kt, page_table, n_fp):
    depth, _, a, page = cache_kt.shape
    db = page_table.shape[0]
    ppb = MOBA_BLOCK // page
    gblk = KMEAN_PAGES // ppb
    assert n_fp % gblk == 0
    grid_spec = pltpu.PrefetchScalarGridSpec(
        num_scalar_prefetch=1,
        grid=(depth, db, n_fp // gblk),
        in_specs=[pl.BlockSpec(memory_space=pl.ANY)],
        out_specs=pl.BlockSpec((None, None, a, n_fp), lambda l, b, t, pt: (l, b, 0, 0)),
        scratch_shapes=[pltpu.VMEM((2, KMEAN_PAGES, a, page), F32), pltpu.SemaphoreType.DMA((2,))])
    return pl.pallas_call(
        _kmean_kernel,
        out_shape=jax.ShapeDtypeStruct((depth, db, a, n_fp), F32),
        grid_spec=grid_spec,
        compiler_params=_cparams(("arbitrary", "arbitrary", "arbitrary")),
        name="kmean_sample",
    )(page_table, cache_kt)


def _spre_kernel(x_ref, sh_ref, sc_ref, g_ref, w_ref, o_ref):
    h = _rms(x_ref[...], g_ref[...]) * (1.0 + sc_ref[...]) + sh_ref[...]
    o_ref[...] = jnp.dot(h.astype(BF16), w_ref[...], preferred_element_type=F32)


def _sample_pre(x, sh, sc, g, w_in_b):
    db, d = x.shape
    n = w_in_b.shape[1]
    full = lambda shape: pl.BlockSpec(shape, lambda i: (0,) * len(shape))
    return pl.pallas_call(
        _spre_kernel,
        out_shape=jax.ShapeDtypeStruct((db, n), F32),
        grid=(1,),
        in_specs=[full((db, d)), full((db, d)), full((db, d)), full((1, d)), full((d, n))],
        out_specs=full((db, n)),
        compiler_params=_cparams(("arbitrary",)),
        name="sample_pre",
    )(x, sh, sc, g, w_in_b)


def _sgate_kernel(q_ref, km_ref, o_ref):
    a, n_fp = km_ref.shape
    n_heads = a // HEAD_DIM
    hrow = lax.broadcasted_iota(jnp.int32, (n_heads, a), 0)
    lane = lax.broadcasted_iota(jnp.int32, (n_heads, a), 1)
    qrows = jnp.where(lane // HEAD_DIM == hrow, jnp.broadcast_to(q_ref[...], (n_heads, a)), 0.0).astype(BF16)
    gate = jnp.dot(qrows, km_ref[...].astype(BF16), preferred_element_type=F32)
    jl = lax.broadcasted_iota(jnp.int32, gate.shape, 1)
    ol = lax.broadcasted_iota(jnp.int32, o_ref.shape, 1)
    out = jnp.zeros(o_ref.shape, jnp.int32)
    for r in range(MOBA_TOPK):
        mx = jnp.max(gate, axis=1, keepdims=True)
        idx = jnp.min(jnp.where(gate == mx, jl, n_fp), axis=1, keepdims=True)
        out = jnp.where(ol == r, idx, out)
        gate = jnp.where(jl == idx, -jnp.inf, gate)
    o_ref[...] = out


def _sample_gate(q3, kmean_l):
    db, _, a = q3.shape
    n_fp = kmean_l.shape[2]
    n_heads = a // HEAD_DIM
    return pl.pallas_call(
        _sgate_kernel,
        out_shape=jax.ShapeDtypeStruct((db, n_heads, LANES), jnp.int32),
        grid=(db,),
        in_specs=[pl.BlockSpec((None, 1, a), lambda b: (b, 0, 0)),
                  pl.BlockSpec((None, a, n_fp), lambda b: (b, 0, 0))],
        out_specs=pl.BlockSpec((None, n_heads, LANES), lambda b: (b, 0, 0)),
        compiler_params=_cparams(("arbitrary",)),
        name="sample_gate",
    )(q3, kmean_l)


def _sattn_copies(pt_ref, sel_ref, ck_ref, cv_ref, kb_ref, vb_ref, sem_ref, layer, slot, b, n_heads, ppb):
    out = []
    for h in range(n_heads):
        for r in range(MOBA_TOPK):
            blk = sel_ref[b, h * MOBA_TOPK + r]
            for pg in range(ppb):
                phys = pt_ref[b, blk * ppb + pg]
                i = r * ppb + pg
                out.append(pltpu.make_async_copy(ck_ref.at[layer, phys, h], kb_ref.at[slot, h, i], sem_ref.at[slot]))
                out.append(pltpu.make_async_copy(cv_ref.at[layer, phys, h], vb_ref.at[slot, h, i], sem_ref.at[slot]))
    return out


def _sattn_kernel(pt_ref, sel_ref, q_ref, k_ref, v_ref, ck_ref, cv_ref, o_ref, kb_ref, vb_ref, sem_ref, *, layer, past_len):
    b = pl.program_id(0)
    n_b = pl.num_programs(0)
    slot = b % 2
    n_heads, n_pg, hd, page = kb_ref.shape[1:]
    ppb = MOBA_BLOCK // page
    args = (pt_ref, sel_ref, ck_ref, cv_ref, kb_ref, vb_ref, sem_ref, layer)

    @pl.when(b == 0)
    def _():
        for c in _sattn_copies(*args, 0, b, n_heads, ppb):
            c.start()

    @pl.when(b + 1 < n_b)
    def _():
        for c in _sattn_copies(*args, 1 - slot, b + 1, n_heads, ppb):
            c.start()

    for c in _sattn_copies(*args, slot, b, n_heads, ppb):
        c.wait()

    n_sel = n_pg * page
    kpos = lax.broadcasted_iota(jnp.int32, (1, n_sel), 1)
    outs = []
    for h in range(n_heads):
        hs = slice(h * hd, (h + 1) * hd)
        slope = 2.0 ** (-8.0 * (h + 1.0) / n_heads)
        qf = (q_ref[:, hs] * HEAD_DIM ** -0.5).astype(BF16)
        q8 = jnp.broadcast_to(qf, (8, hd))
        l_sel = jnp.concatenate([jnp.dot(q8, kb_ref[slot, h, i].astype(BF16), preferred_element_type=F32)
                                 for i in range(n_pg)], axis=1)[0:1]
        blk_of = jnp.zeros((1, n_sel), jnp.int32)
        for r in range(MOBA_TOPK):
            blk_of = jnp.where(kpos // MOBA_BLOCK == r, sel_ref[b, h * MOBA_TOPK + r], blk_of)
        pos_sel = blk_of * MOBA_BLOCK + kpos % MOBA_BLOCK
        l_sel = l_sel - slope * (past_len - pos_sel).astype(F32)
        kown = k_ref[:, hs].astype(BF16).astype(F32)
        vown = v_ref[:, hs].astype(BF16).astype(F32)
        l_own = jnp.sum(qf.astype(F32) * kown, axis=1, keepdims=True)
        mx = jnp.maximum(jnp.max(l_sel, axis=1, keepdims=True), l_own)
        e_sel = jnp.exp(l_sel - mx)
        e_own = jnp.exp(l_own - mx)
        den = jnp.sum(e_sel, axis=1, keepdims=True) + e_own
        p8 = jnp.broadcast_to((e_sel / den).astype(BF16), (8, n_sel))
        p_own = (e_own / den).astype(BF16).astype(F32)
        o_h = p_own * vown
        for i in range(n_pg):
            o_h = o_h + lax.dot_general(p8[:, i * page:(i + 1) * page], vb_ref[slot, h, i].astype(BF16), _NT,
                                        preferred_element_type=F32)[0:1]
        outs.append(o_h)
    o_ref[...] = jnp.concatenate(outs, axis=1)


def _sample_attn(page_table, sel, q3, k3, v3, cache_kt, cache_vt, layer, past_len):
    db, _, a = q3.shape
    n_heads, hd, page = cache_kt.shape[2:]
    ppb = MOBA_BLOCK // page
    row = pl.BlockSpec((None, 1, a), lambda b, pt, sl: (b, 0, 0))
    anyspec = pl.BlockSpec(memory_space=pl.ANY)
    grid_spec = pltpu.PrefetchScalarGridSpec(
        num_scalar_prefetch=2,
        grid=(db,),
        in_specs=[row, row, row, anyspec, anyspec],
        out_specs=row,
        scratch_shapes=[pltpu.VMEM((2, n_heads, MOBA_TOPK * ppb, hd, page), F32),
                        pltpu.VMEM((2, n_heads, MOBA_TOPK * ppb, hd, page), F32),
                        pltpu.SemaphoreType.DMA((2,))])
    return pl.pallas_call(
        functools.partial(_sattn_kernel, layer=layer, past_len=past_len),
        out_shape=jax.ShapeDtypeStruct((db, 1, a), F32),
        grid_spec=grid_spec,
        compiler_params=_cparams(("arbitrary",)),
        name="sample_attn",
    )(page_table, sel, q3, k3, v3, cache_kt, cache_vt)


def _spost_kernel(x_ref, attn_ref, u_ref, st_ref, ga_ref, sh_ref, sc_ref, wp_ref, ps_ref, wo_ref, g_ref, wr_ref,
                  xo_ref, h2_ref, lg_ref, *, past_len):
    a = attn_ref.shape[-1]
    u = u_ref[...]
    st = st_ref[...]
    n_st = st.shape[1]
    gw = a // len(POOL_WINDOWS)
    srow = lax.broadcasted_iota(jnp.int32, (1, n_st, a), 1)
    lane = lax.broadcasted_iota(jnp.int32, (1, n_st, a), 2)
    win = jnp.zeros((1, n_st, a), jnp.int32)
    cnt = jnp.zeros((1, a), F32)
    lane2 = lax.broadcasted_iota(jnp.int32, (1, a), 1)
    for g, w in enumerate(POOL_WINDOWS):
        win = jnp.where(lane // gw == g, w, win)
        cnt = jnp.where(lane2 // gw == g, float(min(w, past_len + 1)), cnt)
    wsum = jnp.sum(jnp.where(srow >= n_st + 1 - win, st, 0.0), axis=1) + u
    diff = (wsum / cnt - u).astype(BF16)
    pools = [jnp.dot(diff[:, g * gw:(g + 1) * gw], wp_ref[g], preferred_element_type=F32)
             for g in range(len(POOL_WINDOWS))]
    pool = jnp.concatenate(pools, axis=1) * ps_ref[...]
    mo = (jnp.dot(attn_ref[...].astype(BF16), wo_ref[0:a, :], preferred_element_type=F32)
          + jnp.dot(pool.astype(BF16), wo_ref[a:, :], preferred_element_type=F32))
    xn = x_ref[...] + ga_ref[...] * mo
    xo_ref[...] = xn
    hb = (_rms(xn, g_ref[...]) * (1.0 + sc_ref[...]) + sh_ref[...]).astype(BF16)
    h2_ref[...] = hb
    lg_ref[...] = lax.dot_general(wr_ref[...], hb, _NT, preferred_element_type=F32)


def _sample_post(x, attn, u, state, ga1, sh2, sc2, w_pool_b, pool_scale, w_out_b, g_ffn, wr_b, past_len):
    db, d = x.shape
    ne = wr_b.shape[0]
    ins = (x, attn, u, state, ga1, sh2, sc2, w_pool_b, pool_scale, w_out_b, g_ffn, wr_b)
    full = lambda arr: pl.BlockSpec(arr.shape, lambda i, n=arr.ndim: (0,) * n)
    outs = (jax.ShapeDtypeStruct((db, d), F32), jax.ShapeDtypeStruct((db, d), BF16), jax.ShapeDtypeStruct((ne, db), F32))
    return pl.pallas_call(
        functools.partial(_spost_kernel, past_len=past_len),
        out_shape=outs,
        grid=(1,),
        in_specs=[full(x_) for x_ in ins],
        out_specs=tuple(full(o) for o in outs),
        compiler_params=_cparams(("arbitrary",)),
        name="sample_post",
    )(*ins)


def _smoe_kernel(h_ref, cb_ref, w1_ref, w3_ref, w2_ref, xn_ref, ga_ref, o_ref, acc_ref):
    e = pl.program_id(0)

    @pl.when(e == 0)
    def _():
        acc_ref[...] = jnp.zeros_like(acc_ref)

    h = h_ref[...]
    a = jnp.dot(h, w1_ref[...], preferred_element_type=F32)
    g = jnp.dot(h, w3_ref[...], preferred_element_type=F32)
    y = jnp.dot(((a * jax.nn.sigmoid(a)) * g).astype(BF16), w2_ref[...], preferred_element_type=F32)
    acc_ref[...] += y.astype(BF16).astype(F32) * cb_ref[...].astype(BF16).astype(F32)

    @pl.when(e == pl.num_programs(0) - 1)
    def _():
        o_ref[...] = xn_ref[...] + ga_ref[...] * acc_ref[...]


def _sample_moe(h2b, comb, w1_b, w3_b, w2_b, xn, ga2):
    db, d = h2b.shape
    n_e, _, de = w1_b.shape
    cst = lambda e: (0, 0)
    return pl.pallas_call(
        _smoe_kernel,
        out_shape=jax.ShapeDtypeStruct((db, d), F32),
        grid=(n_e,),
        in_specs=[pl.BlockSpec((db, d), cst),
                  pl.BlockSpec((None, db, 1), lambda e: (e, 0, 0)),
                  pl.BlockSpec((None, d, de), lambda e: (e, 0, 0)),
                  pl.BlockSpec((None, d, de), lambda e: (e, 0, 0)),
                  pl.BlockSpec((None, de, d), lambda e: (e, 0, 0)),
                  pl.BlockSpec((db, d), cst),
                  pl.BlockSpec((db, d), cst)],
        out_specs=pl.BlockSpec((db, d), cst),
        scratch_shapes=[pltpu.VMEM((db, d), F32)],
        compiler_params=_cparams(("arbitrary",)),
        name="sample_moe",
    )(h2b, comb, w1_b, w3_b, w2_b, xn, ga2)


def _sample_layer(x, mod_s, layer, kmean_l, cache_kt, cache_vt, page_table, state_l, g_mix_l, w_in_b_l, w_pool_b_l,
                  pool_scale_l, w_out_b_l, g_ffn_l, wr_b, router_bias, w1_b_l, w3_b_l, w2_b_l, past_len):
    db, d = x.shape
    a = kmean_l.shape[1]
    n_e = w1_b_l.shape[0]
    sh1, sc1, ga1, sh2, sc2, ga2 = [mod_s[:, i] for i in range(6)]
    r = _sample_pre(x, sh1, sc1, g_mix_l, w_in_b_l)
    q3, k3, v3, u = r[:, None, :a], r[:, None, a:2 * a], r[:, None, 2 * a:3 * a], r[:, 3 * a:]
    sel = _sample_gate(q3, kmean_l)[:, :, :MOBA_TOPK].reshape(db, -1)
    attn = _sample_attn(page_table, sel, q3, k3, v3, cache_kt, cache_vt, layer, past_len)[:, 0]
    xn, h2b, lgt = _sample_post(x, attn, u, state_l, ga1, sh2, sc2, w_pool_b_l, pool_scale_l, w_out_b_l,
                                g_ffn_l, wr_b, past_len)
    idx, _, gate, _ = _route(lgt[None], router_bias, db)
    comb = jnp.sum(jnp.where(idx[None, :, :] == jnp.arange(n_e)[:, None, None], gate[None], 0.0), axis=-1)
    x_out = _sample_moe(h2b, comb[..., None], w1_b_l, w3_b_l, w2_b_l, xn, ga2)
    return x_out, k3, v3, u


def kernel(x_prompt, x_sample, cache_k, cache_v, state_pool, page_table, c_prompt, c_sample,
           w_ada, b_ada, g_mix, w_in, w_pool, pool_scale, w_out, g_ffn, w_router, router_bias,
           w1, w3, w2, g_final):
    b, l, d = x_prompt.shape
    db, ds, _ = x_sample.shape
    depth = w_ada.shape[0]
    n_heads = cache_k.shape[3]
    a = n_heads * HEAD_DIM
    n_experts = w1.shape[1]
    tm = 512
    assert l % tm == 0 and tm % MOBA_BLOCK == 0 and a % LANES == 0 and n_experts <= 16

    mod = _ada(jnp.concatenate([c_prompt, c_sample], axis=0), w_ada, b_ada)
    mod = mod.reshape(depth, b + db, 6, d)
    slopes = _alibi_slopes(n_heads)

    w_q, w_k, w_v = w_in[:, :, :a], w_in[:, :, a:2 * a], w_in[:, :, 2 * a:3 * a]

    def head_pad(w):
        w = w.reshape(depth, d, n_heads, HEAD_DIM)
        return jnp.pad(w, ((0, 0), (0, 0), (0, 0), (0, HEAD_PAD - HEAD_DIM))).reshape(depth, d, n_heads * HEAD_PAD)

    wn_b = w_in[:, :, a:].astype(BF16)
    wa_b = jnp.concatenate([head_pad(w_q), head_pad(w_k)], axis=-1).astype(BF16)
    wvt_b = jnp.swapaxes(w_v, 1, 2).astype(BF16)
    slopes_np = 2.0 ** (-8.0 * (np.arange(n_heads, dtype=np.float64) + 1.0) / n_heads)
    key_term = (slopes_np[:, None] * np.arange(MOBA_BLOCK)[None, :]).astype(np.float32)
    assert np.array_equal(key_term.astype(BF16).astype(np.float32), key_term), \
        "ALiBi key term must be exact in bfloat16 to ride in the padded key lanes"
    alibi_lane = np.zeros((n_heads, HEAD_PAD), np.float32)
    alibi_lane[:, HEAD_DIM] = 1.0
    orow = jnp.asarray(alibi_lane.reshape(1, -1))
    srow = jnp.asarray((alibi_lane * slopes_np[:, None].astype(np.float32)).reshape(1, -1))
    w_out_b = w_out.astype(BF16)
    w_pool_b = w_pool.astype(BF16)
    w1_b, w3_b, w2_b = w1.astype(BF16), w3.astype(BF16), w2.astype(BF16)
    wr_b = w_router.T.astype(BF16)

    w_in_b = w_in.astype(BF16)
    page = cache_k.shape[2]
    n_pool = cache_k.shape[1]
    past_len = page_table.shape[1] * page
    n_fp = past_len // MOBA_BLOCK
    assert ds == 1 and past_len % MOBA_BLOCK == 0 and n_fp >= MOBA_TOPK and MOBA_BLOCK % page == 0
    cache_kt = jnp.transpose(cache_k, (0, 1, 3, 4, 2))
    cache_vt = jnp.transpose(cache_v, (0, 1, 3, 4, 2))
    kmean_s = _kmean_sample(cache_kt.reshape(depth, n_pool, a, page), page_table, n_fp)

    xp, xs = x_prompt, x_sample.reshape(db, d)
    kp, vp, up, kq, vq, uq = [], [], [], [], [], []
    for li in range(depth):
        mod_p = mod[li, :b]
        k, v, u, qa, ka, vt, kmean = _qkvu(xp, mod_p, g_mix[li][None], wn_b[li], wa_b[li], wvt_b[li], srow, orow, tm)
        kp.append(k.reshape(b, l, n_heads, HEAD_DIM))
        vp.append(v.reshape(b, l, n_heads, HEAD_DIM))
        attn = _moba_prompt(slopes, qa, ka, vt, kmean)
        xn, h2, lg = _mix(xp, attn, u, mod_p, w_pool_b[li], pool_scale[li][None], w_out_b[li],
                          g_ffn[li][None], wr_b, tm)
        yb, dest2, gate = _moe(h2.reshape(b * l, d), lg, router_bias, w1_b[li], w3_b[li], w2_b[li], tm)
        xp = _combine(dest2, yb, xn, mod_p, gate, g_final[None], MOE_TOK, li == depth - 1)
        up.append(u[:, l - POOL_STATE_LEN:])

        xs, k3, v3, u_s = _sample_layer(
            xs, mod[li, b:], li, kmean_s[li], cache_kt, cache_vt, page_table, state_pool[li], g_mix[li][None],
            w_in_b[li], w_pool_b[li], pool_scale[li][None], w_out_b[li], g_ffn[li][None], wr_b, router_bias,
            w1_b[li], w3_b[li], w2_b[li], past_len)
        kq.append(k3.reshape(db, ds, n_heads, HEAD_DIM))
        vq.append(v3.reshape(db, ds, n_heads, HEAD_DIM))
        uq.append(jnp.concatenate([state_pool[li][:, ds:], u_s[:, None]], axis=1))

    y_prompt = xp
    y_sample = _final_norm(xs, g_final[None]).reshape(db, ds, d)
    return (y_prompt, y_sample, jnp.stack(kp), jnp.stack(vp), jnp.stack(up),
            jnp.stack(kq), jnp.stack(vq), jnp.stack(uq))
```

```python
import functools

import numpy as np
import jax
import jax.numpy as jnp
from jax import lax
from jax.experimental import pallas as pl
from jax.experimental.pallas import tpu as pltpu

F32 = jnp.float32
BF16 = jnp.bfloat16

HEAD_DIM = 64
MOBA_BLOCK = 256
MOBA_TOPK = 3
POOL_WINDOWS = (2, 4, 8, 16)
POOL_STATE_LEN = max(POOL_WINDOWS) - 1
POOL_HALO = 16
N_EXPERT_GROUPS = 4
TOP_K = 2
NORM_EPS = 1e-6
LANES = 128
HEAD_PAD = 128
ATTN_GROUP = 4
ATTN_HEADS = 4
KMEAN_PAGES = 16
MOE_TOK = 256
MOE_ROWS = 256
VMEM_LIMIT = 48 * 1024 * 1024

_NT = (((1,), (1,)), ((), ()))
_TN = (((0,), (0,)), ((), ()))


def _cparams(sem):
    return pltpu.CompilerParams(dimension_semantics=sem, vmem_limit_bytes=VMEM_LIMIT)


def _rms(x, g):
    return x * lax.rsqrt(jnp.mean(x * x, axis=-1, keepdims=True) + NORM_EPS) * g


def _ada_kernel(c_ref, w_ref, b_ref, o_ref):
    c = c_ref[...]
    s = c * jax.nn.sigmoid(c)
    o_ref[...] = jnp.dot(s.astype(BF16), w_ref[...].astype(BF16), preferred_element_type=F32) + b_ref[...]


def _ada(c, w_ada, b_ada):
    depth, d, n = w_ada.shape
    nb = c.shape[0]
    tn = 1536
    return pl.pallas_call(
        _ada_kernel,
        out_shape=jax.ShapeDtypeStruct((depth, nb, n), F32),
        grid=(depth, n // tn),
        in_specs=[pl.BlockSpec((nb, d), lambda l, j: (0, 0)),
                  pl.BlockSpec((None, d, tn), lambda l, j: (l, 0, j)),
                  pl.BlockSpec((None, 1, tn), lambda l, j: (l, 0, j))],
        out_specs=pl.BlockSpec((None, nb, tn), lambda l, j: (l, 0, j)),
        compiler_params=_cparams(("arbitrary", "arbitrary")),
        name="ada",
    )(c, w_ada, b_ada.reshape(depth, 1, n))


def _qkvu_kernel(x_ref, mod_ref, g_ref, wn_ref, wa_ref, wvt_ref, srow_ref, orow_ref,
                 k_ref, v_ref, u_ref, qa_ref, ka_ref, vt_ref, km_ref):
    a = k_ref.shape[-1]
    tm = x_ref.shape[0]
    ap = qa_ref.shape[-1]
    h = _rms(x_ref[...], g_ref[...]) * (1.0 + mod_ref[1:2, :]) + mod_ref[0:1, :]
    hb = h.astype(BF16)
    rn = jnp.dot(hb, wn_ref[...], preferred_element_type=F32)
    k_ref[...] = rn[:, :a]
    v_ref[...] = rn[:, a:2 * a]
    u_ref[...] = rn[:, 2 * a:]
    ra = jnp.dot(hb, wa_ref[...], preferred_element_type=F32)
    qa_ref[...] = (ra[:, :ap] * HEAD_DIM ** -0.5 + orow_ref[...]).astype(BF16)
    kp = ra[:, ap:]
    ridx = (lax.broadcasted_iota(jnp.int32, (tm, 1), 0) % MOBA_BLOCK).astype(F32)
    ka_ref[...] = (kp + ridx * srow_ref[...]).astype(BF16)
    vt = lax.dot_general(wvt_ref[...], hb, _NT, preferred_element_type=F32)
    for s in range(km_ref.shape[0]):
        rows = slice(s * MOBA_BLOCK, (s + 1) * MOBA_BLOCK)
        km_ref[s:s + 1, :] = jnp.mean(kp[rows], axis=0, keepdims=True)
        vt_ref[s] = vt[:, rows].astype(BF16)


def _qkvu(x, mod, g, wn_b, wa_b, wvt_b, srow, orow, tm):
    b, l, d = x.shape
    a = wn_b.shape[1] // 3
    ap = wa_b.shape[1] // 2
    nkb_t = tm // MOBA_BLOCK
    tok = lambda bi, t: (bi, t, 0)
    cst = lambda bi, t: (0, 0)
    outs = pl.pallas_call(
        _qkvu_kernel,
        out_shape=(jax.ShapeDtypeStruct((b, l, a), F32),
                   jax.ShapeDtypeStruct((b, l, a), F32),
                   jax.ShapeDtypeStruct((b, l, a), F32),
                   jax.ShapeDtypeStruct((b, l, ap), BF16),
                   jax.ShapeDtypeStruct((b, l, ap), BF16),
                   jax.ShapeDtypeStruct((b, l // MOBA_BLOCK, a, MOBA_BLOCK), BF16),
                   jax.ShapeDtypeStruct((b, l // tm, nkb_t, ap), F32)),
        grid=(b, l // tm),
        in_specs=[pl.BlockSpec((None, tm, d), tok),
                  pl.BlockSpec((None, 6, d), lambda bi, t: (bi, 0, 0)),
                  pl.BlockSpec((1, d), cst),
                  pl.BlockSpec((d, 3 * a), cst),
                  pl.BlockSpec((d, 2 * ap), cst),
                  pl.BlockSpec((a, d), cst),
                  pl.BlockSpec((1, ap), cst),
                  pl.BlockSpec((1, ap), cst)],
        out_specs=(pl.BlockSpec((None, tm, a), tok),) * 3 + (pl.BlockSpec((None, tm, ap), tok),) * 2
        + (pl.BlockSpec((None, nkb_t, a, MOBA_BLOCK), lambda bi, t: (bi, t, 0, 0)),
           pl.BlockSpec((None, None, nkb_t, ap), lambda bi, t: (bi, t, 0, 0))),
        compiler_params=_cparams(("arbitrary", "arbitrary")),
        name="qkvu",
    )(x, mod, g, wn_b, wa_b, wvt_b, srow, orow)
    k, v, u, qa, ka, vt, km = outs
    return k, v, u, qa, ka, vt, km.reshape(b, l // MOBA_BLOCK, ap)


def _cache_unit_copies(pt_ref, cache_ref, buf_ref, sem_ref, layer, slot, unit, units_per_row):
    row, t = unit // units_per_row, unit % units_per_row
    return [pltpu.make_async_copy(cache_ref.at[layer, pt_ref[row, t * KMEAN_PAGES + pg]], buf_ref.at[slot, pg],
                                  sem_ref.at[slot]) for pg in range(KMEAN_PAGES)]


def _attn_kernel(sl_ref, pt_ref, q_ref, k_ref, vt_ref, km_ref, cache_ref, o_ref, kmo_ref, seln_ref, cbuf_ref, csem_ref,
                 *, layer, n_units, units_per_row):
    p = pl.program_id(1)
    i = pl.program_id(2)
    blk = MOBA_BLOCK
    n_kb = km_ref.shape[0]
    nh = ATTN_HEADS

    step = (pl.program_id(0) * pl.num_programs(1) + p) * pl.num_programs(2) + i
    n_steps = pl.num_programs(0) * pl.num_programs(1) * pl.num_programs(2)
    slot = step % 2
    unit = jnp.minimum(step, n_units - 1)
    cargs = (pt_ref, cache_ref, cbuf_ref, csem_ref, layer)

    @pl.when(step == 0)
    def _():
        for c in _cache_unit_copies(*cargs, 0, unit, units_per_row):
            c.start()

    @pl.when(step + 1 < n_steps)
    def _():
        for c in _cache_unit_copies(*cargs, 1 - slot, jnp.minimum(step + 1, n_units - 1), units_per_row):
            c.start()

    t = unit % units_per_row

    @pl.when(t == 0)
    def _():
        kmo_ref[...] = jnp.zeros_like(kmo_ref)

    for c in _cache_unit_copies(*cargs, slot, unit, units_per_row):
        c.wait()
    ppb = MOBA_BLOCK // cbuf_ref.shape[3]
    gblk = KMEAN_PAGES // ppb
    klane = lax.broadcasted_iota(jnp.int32, kmo_ref.shape, 1)
    kacc = kmo_ref[...]
    for g in range(gblk):
        tot = cbuf_ref[slot, g * ppb]
        for q in range(1, ppb):
            tot = tot + cbuf_ref[slot, g * ppb + q]
        col = jnp.sum(tot, axis=1, keepdims=True) * (1.0 / MOBA_BLOCK)
        kacc = jnp.where(klane == t * gblk + g, col, kacc)
    kmo_ref[...] = kacc

    kidx = lax.broadcasted_iota(jnp.int32, (blk, blk), 0)
    qidx = lax.broadcasted_iota(jnp.int32, (blk, blk), 1)
    causal = kidx <= qidx
    jrow = lax.broadcasted_iota(jnp.int32, (n_kb, blk), 0)
    start = pl.multiple_of(i * blk, blk)
    hls = [slice(s * HEAD_PAD, (s + 1) * HEAD_PAD) for s in range(nh)]
    hrs = [slice(s * HEAD_DIM, (s + 1) * HEAD_DIM) for s in range(nh)]
    slopes = [sl_ref[nh * p + s] for s in range(nh)]
    init = []
    for s in range(nh):
        qh = q_ref[:, hls[s]]
        gate = lax.dot_general(km_ref[:, hls[s]].astype(BF16), qh, _NT, preferred_element_type=F32)
        gate = jnp.where(jrow < i, gate, -jnp.inf)
        sel = jnp.zeros(gate.shape, jnp.bool_)
        for r in range(MOBA_TOPK):
            mx = jnp.max(gate, axis=0, keepdims=True)
            idx = jnp.min(jnp.where(gate == mx, jrow, n_kb), axis=0, keepdims=True)
            pick = jrow == idx
            sel = sel | (pick & (i > r))
            gate = jnp.where(pick, -jnp.inf, gate)
        seln_ref[s] = jnp.where(sel, 0.0, -jnp.inf)
        st = lax.dot_general(k_ref[pl.ds(start, blk), hls[s]], qh, _NT, preferred_element_type=F32)
        st = jnp.where(causal, st, -jnp.inf)
        m0 = jnp.max(st, axis=0, keepdims=True)
        pt = jnp.exp(st - m0)
        l0 = jnp.sum(pt, axis=0, keepdims=True)
        acc0 = jnp.dot(vt_ref[i, hrs[s], :], pt.astype(BF16), preferred_element_type=F32)
        init.append((m0, l0, acc0))

    def scores(g, s, m_old):
        qh = q_ref[:, hls[s]]
        sts, offs = [], []
        m_new = m_old
        for u in range(ATTN_GROUP):
            j = g * ATTN_GROUP + u
            ks = pl.multiple_of(j * blk, blk)
            st_u = lax.dot_general(k_ref[pl.ds(ks, blk), hls[s]], qh, _NT, preferred_element_type=F32)
            cj = slopes[s] * ((j - i) * blk).astype(F32)
            off = cj + seln_ref[s, pl.ds(j, 1), :]
            m_new = jnp.maximum(m_new, jnp.max(st_u, axis=0, keepdims=True) + off)
            sts.append(st_u)
            offs.append(off)
        return sts, offs, m_new

    def accumulate(g, s, state, staged):
        m_old, l_old, acc_old = state
        sts, offs, m_new = staged
        alpha = jnp.exp(m_old - m_new)
        l_new = alpha * l_old
        acc_new = alpha * acc_old
        for u in range(ATTN_GROUP):
            j = g * ATTN_GROUP + u
            pt_u = jnp.exp(sts[u] - (m_new - offs[u]))
            l_new = l_new + jnp.sum(pt_u, axis=0, keepdims=True)
            acc_new = acc_new + jnp.dot(vt_ref[j, hrs[s], :], pt_u.astype(BF16), preferred_element_type=F32)
        return m_new, l_new, acc_new

    def group(g, carry):
        staged = scores(g, 0, carry[0][0])
        out = []
        for s in range(nh):
            nxt = scores(g, s + 1, carry[s + 1][0]) if s + 1 < nh else None
            out.append(accumulate(g, s, carry[s], staged))
            staged = nxt
        return tuple(out)

    final = lax.fori_loop(0, (i + ATTN_GROUP - 1) // ATTN_GROUP, group, tuple(init))
    ot = jnp.concatenate([acc_f / l_f for (_, l_f, acc_f) in final], axis=0)
    o_ref[...] = ot.T.astype(o_ref.dtype)


def _moba_prompt(slopes, page_table, qa, ka, vt, kmean, cache_kt, layer, n_fp):
    b, l, ap = qa.shape
    n_kb = l // MOBA_BLOCK
    blk = MOBA_BLOCK
    nh = ATTN_HEADS
    n_hg = ap // (nh * HEAD_PAD)
    _, _, a, page = cache_kt.shape
    db = page_table.shape[0]
    gblk = KMEAN_PAGES // (MOBA_BLOCK // page)
    units_per_row = n_fp // gblk
    n_units = db * units_per_row
    n_steps = b * n_hg * n_kb
    assert n_kb % ATTN_GROUP == 0 and ap % (nh * HEAD_PAD) == 0 and n_fp % gblk == 0 and n_units <= n_steps

    def kmo_index(bi, p, i, sl, pt):
        step = (bi * n_hg + p) * n_kb + i
        return (jnp.minimum(step, n_units - 1) // units_per_row, 0, 0)

    grid_spec = pltpu.PrefetchScalarGridSpec(
        num_scalar_prefetch=2,
        grid=(b, n_hg, n_kb),
        in_specs=[pl.BlockSpec((None, blk, nh * HEAD_PAD), lambda bi, p, i, sl, pt: (bi, i, p)),
                  pl.BlockSpec((None, l, nh * HEAD_PAD), lambda bi, p, i, sl, pt: (bi, 0, p)),
                  pl.BlockSpec((None, n_kb, nh * HEAD_DIM, blk), lambda bi, p, i, sl, pt: (bi, 0, p, 0)),
                  pl.BlockSpec((None, n_kb, nh * HEAD_PAD), lambda bi, p, i, sl, pt: (bi, 0, p)),
                  pl.BlockSpec(memory_space=pl.ANY)],
        out_specs=(pl.BlockSpec((None, blk, nh * HEAD_DIM), lambda bi, p, i, sl, pt: (bi, i, p)),
                   pl.BlockSpec((None, a, n_fp), kmo_index)),
        scratch_shapes=[pltpu.VMEM((nh, n_kb, blk), F32),
                        pltpu.VMEM((2, KMEAN_PAGES, a, page), F32),
                        pltpu.SemaphoreType.DMA((2,))])
    return pl.pallas_call(
        functools.partial(_attn_kernel, layer=layer, n_units=n_units, units_per_row=units_per_row),
        out_shape=(jax.ShapeDtypeStruct((b, l, n_hg * nh * HEAD_DIM), BF16),
                   jax.ShapeDtypeStruct((db, a, n_fp), F32)),
        grid_spec=grid_spec,
        compiler_params=_cparams(("arbitrary", "arbitrary", "arbitrary")),
        name="moba_prompt",
    )(slopes, page_table, qa, ka, vt, kmean, cache_kt)


def _mix_kernel(x_ref, attn_ref, u_ref, halo_ref, mod_ref, wp_ref, ps_ref, wo_ref, g_ref, wr_ref,
                xo_ref, h2_ref, lg_ref, ext_ref):
    t = pl.program_id(1)
    tm = x_ref.shape[0]
    a = attn_ref.shape[-1]
    halo = jnp.where(t == 0, 0.0, halo_ref[...])
    ext_ref[0:POOL_HALO, :] = halo
    ext_ref[POOL_HALO:POOL_HALO + tm, :] = u_ref[...]
    pos = t * tm + lax.broadcasted_iota(jnp.int32, (tm, 1), 0)
    gw = a // len(POOL_WINDOWS)
    pools = []
    for g, w in enumerate(POOL_WINDOWS):
        ln = slice(g * gw, (g + 1) * gw)
        acc = ext_ref[POOL_HALO:POOL_HALO + tm, ln]
        for s in range(1, w):
            acc = acc + ext_ref[POOL_HALO - s:POOL_HALO - s + tm, ln]
        cnt = jnp.minimum(w, pos + 1).astype(F32)
        diff = acc / cnt - u_ref[:, ln]
        pools.append(jnp.dot(diff.astype(BF16), wp_ref[g], preferred_element_type=F32))
    pool = jnp.concatenate(pools, axis=1) * ps_ref[...]
    mo = (jnp.dot(attn_ref[...], wo_ref[0:a, :], preferred_element_type=F32)
          + jnp.dot(pool.astype(BF16), wo_ref[a:, :], preferred_element_type=F32))
    xn = x_ref[...] + mod_ref[2:3, :] * mo
    xo_ref[...] = xn
    h2 = _rms(xn, g_ref[...]) * (1.0 + mod_ref[4:5, :]) + mod_ref[3:4, :]
    h2_ref[...] = h2
    lg_ref[...] = lax.dot_general(wr_ref[...], h2.astype(BF16), _NT, preferred_element_type=F32)


def _mix(x, attn, u, mod, w_pool_b, pool_scale, w_out_b, g_ffn, wr_b, tm):
    b, l, d = x.shape
    a = attn.shape[-1]
    ne = wr_b.shape[0]
    tok = lambda bi, t: (bi, t, 0)
    cst2 = lambda bi, t: (0, 0)
    hb = tm // POOL_HALO
    return pl.pallas_call(
        _mix_kernel,
        out_shape=(jax.ShapeDtypeStruct((b, l, d), F32),
                   jax.ShapeDtypeStruct((b, l, d), F32),
                   jax.ShapeDtypeStruct((b, ne, l), F32)),
        grid=(b, l // tm),
        in_specs=[pl.BlockSpec((None, tm, d), tok),
                  pl.BlockSpec((None, tm, a), tok),
                  pl.BlockSpec((None, tm, a), tok),
                  pl.BlockSpec((None, POOL_HALO, a), lambda bi, t: (bi, jnp.maximum(t * hb - 1, 0), 0)),
                  pl.BlockSpec((None, 6, d), lambda bi, t: (bi, 0, 0)),
                  pl.BlockSpec(w_pool_b.shape, lambda bi, t: (0, 0, 0)),
                  pl.BlockSpec((1, a), cst2),
                  pl.BlockSpec((d, d), cst2),
                  pl.BlockSpec((1, d), cst2),
                  pl.BlockSpec((ne, d), cst2)],
        out_specs=(pl.BlockSpec((None, tm, d), tok),
                   pl.BlockSpec((None, tm, d), tok),
                   pl.BlockSpec((None, ne, tm), lambda bi, t: (bi, 0, t))),
        scratch_shapes=[pltpu.VMEM((tm + POOL_HALO, a), F32)],
        compiler_params=_cparams(("arbitrary", "arbitrary")),
        name="mix",
    )(x, attn, u, u, mod, w_pool_b, pool_scale, w_out_b, g_ffn, wr_b)


def _moe_kernel(be_ref, na_ref, x_ref, w1_ref, w3_ref, w2_ref, o_ref):
    i = pl.program_id(0)

    @pl.when(i < na_ref[0])
    def _():
        x = x_ref[...].astype(BF16)
        a = jnp.dot(x, w1_ref[...], preferred_element_type=F32)
        g = jnp.dot(x, w3_ref[...], preferred_element_type=F32)
        hmid = (a * jax.nn.sigmoid(a)) * g
        o_ref[...] = jnp.dot(hmid.astype(BF16), w2_ref[...], preferred_element_type=F32)

    @pl.when(i >= na_ref[0])
    def _():
        o_ref[...] = jnp.zeros_like(o_ref)


def _moe_blocks(block_e, n_active, xb, w1_b, w3_b, w2_b):
    cap, d = xb.shape
    de = w1_b.shape[-1]
    n_blocks = cap // MOE_ROWS
    grid_spec = pltpu.PrefetchScalarGridSpec(
        num_scalar_prefetch=2,
        grid=(n_blocks,),
        in_specs=[pl.BlockSpec((MOE_ROWS, d), lambda i, be, na: (i, 0)),
                  pl.BlockSpec((None, d, de), lambda i, be, na: (be[i], 0, 0)),
                  pl.BlockSpec((None, d, de), lambda i, be, na: (be[i], 0, 0)),
                  pl.BlockSpec((None, de, d), lambda i, be, na: (be[i], 0, 0))],
        out_specs=pl.BlockSpec((MOE_ROWS, d), lambda i, be, na: (i, 0)))
    return pl.pallas_call(
        _moe_kernel,
        out_shape=jax.ShapeDtypeStruct((cap, d), F32),
        grid_spec=grid_spec,
        compiler_params=_cparams(("arbitrary",)),
        name="moe_blocks",
    )(block_e, n_active, xb, w1_b, w3_b, w2_b)


def _route_rows(lg, bias):
    n_e = lg.shape[0]
    epg = n_e // N_EXPERT_GROUPS
    sc = jax.nn.sigmoid(lg)
    bi = sc + bias
    s_rows = [sc[e:e + 1, :] for e in range(n_e)]
    b_rows = [bi[e:e + 1, :] for e in range(n_e)]
    gs = []
    for g in range(N_EXPERT_GROUPS):
        v = b_rows[g * epg:(g + 1) * epg]
        best = None
        for a_ in range(epg):
            for b_ in range(a_ + 1, epg):
                pr = v[a_] + v[b_]
                best = pr if best is None else jnp.maximum(best, pr)
        gs.append(best)
    grp = jnp.zeros(gs[0].shape, jnp.int32)
    gbest = gs[0]
    for g in range(1, N_EXPERT_GROUPS):
        up = gs[g] > gbest
        gbest = jnp.where(up, gs[g], gbest)
        grp = jnp.where(up, g, grp)
    c = [b_rows[j] for j in range(epg)]
    d = [s_rows[j] for j in range(epg)]
    for g in range(1, N_EXPERT_GROUPS):
        hit = grp == g
        c = [jnp.where(hit, b_rows[g * epg + j], c[j]) for j in range(epg)]
        d = [jnp.where(hit, s_rows[g * epg + j], d[j]) for j in range(epg)]

    def first_argmax(vals):
        idx = jnp.zeros(vals[0].shape, jnp.int32)
        best = vals[0]
        for j in range(1, len(vals)):
            up = vals[j] > best
            best = jnp.where(up, vals[j], best)
            idx = jnp.where(up, j, idx)
        return idx

    i1 = first_argmax(c)
    i2 = first_argmax([jnp.where(i1 == j, -jnp.inf, c[j]) for j in range(epg)])
    g1 = d[0]
    g2 = d[0]
    for j in range(1, epg):
        g1 = jnp.where(i1 == j, d[j], g1)
        g2 = jnp.where(i2 == j, d[j], g2)
    tot = g1 + g2
    return grp * epg + i1, grp * epg + i2, g1 / tot, g2 / tot


def _route_kernel(lg_ref, bias_ref, tri_ref, io_ref, fo_ref, cnt_ref, run_ref):
    first = (pl.program_id(0) == 0) & (pl.program_id(1) == 0)

    @pl.when(first)
    def _():
        run_ref[...] = jnp.zeros_like(run_ref)

    lg = lg_ref[...]
    n_e, tr = lg.shape
    e0, e1, g0, g1 = _route_rows(lg, bias_ref[...])
    erow = lax.broadcasted_iota(jnp.int32, (n_e, tr), 0)
    oh0 = erow == e0
    oh1 = erow == e1
    both = (oh0 | oh1).astype(F32)
    before = jnp.dot(both.astype(BF16), tri_ref[...], preferred_element_type=F32) + run_ref[...]
    r0 = jnp.sum(jnp.where(oh0, before, 0.0), axis=0, keepdims=True)
    r1 = jnp.sum(jnp.where(oh1, before, 0.0), axis=0, keepdims=True)
    run_ref[...] = run_ref[...] + jnp.sum(both, axis=1, keepdims=True)
    io_ref[...] = jnp.concatenate([e0, e1, r0.astype(jnp.int32), r1.astype(jnp.int32),
                                   jnp.zeros((4, tr), jnp.int32)], axis=0)
    fo_ref[...] = jnp.concatenate([g0, g1, jnp.zeros((6, tr), F32)], axis=0)
    cnt_ref[...] = jnp.broadcast_to(run_ref[...], cnt_ref.shape)


def _route(lgt, router_bias, tr):
    b, n_e, l = lgt.shape
    tri = (jnp.arange(tr)[:, None] < jnp.arange(tr)[None, :]).astype(BF16)
    io, fo, cnt = pl.pallas_call(
        _route_kernel,
        out_shape=(jax.ShapeDtypeStruct((b, 8, l), jnp.int32),
                   jax.ShapeDtypeStruct((b, 8, l), F32),
                   jax.ShapeDtypeStruct((n_e, LANES), F32)),
        grid=(b, l // tr),
        in_specs=[pl.BlockSpec((None, n_e, tr), lambda bi, t: (bi, 0, t)),
                  pl.BlockSpec((n_e, 1), lambda bi, t: (0, 0)),
                  pl.BlockSpec((tr, tr), lambda bi, t: (0, 0))],
        out_specs=(pl.BlockSpec((None, 8, tr), lambda bi, t: (bi, 0, t)),
                   pl.BlockSpec((None, 8, tr), lambda bi, t: (bi, 0, t)),
                   pl.BlockSpec((n_e, LANES), lambda bi, t: (0, 0))),
        scratch_shapes=[pltpu.VMEM((n_e, 1), F32)],
        compiler_params=_cparams(("arbitrary", "arbitrary")),
        name="route",
    )(lgt, router_bias.astype(F32).reshape(n_e, 1), tri)
    idx = jnp.stack([io[:, 0], io[:, 1]], axis=-1).reshape(b * l, TOP_K)
    rank = jnp.stack([io[:, 2], io[:, 3]], axis=-1).reshape(b * l, TOP_K)
    gate = jnp.stack([fo[:, 0], fo[:, 1]], axis=-1).reshape(b * l, TOP_K)
    return idx, rank, gate, cnt[:, 0].astype(jnp.int32)


def _row_copy(src_ref, src_row, dst_ref, dst_row, sem_ref):
    return pltpu.make_async_copy(src_ref.at[pl.ds(src_row, 1)], dst_ref.at[pl.ds(dst_row, 1)], sem_ref.at[0])


def _dispatch_kernel(dest_ref, h_ref, xz_ref, xb_ref, sem_ref):
    del xz_ref
    td = dest_ref.shape[1] // TOP_K

    def issue(r, c):
        for k in range(TOP_K):
            _row_copy(h_ref, r, xb_ref, dest_ref[0, TOP_K * r + k], sem_ref).start(priority=k % 2)
        return c

    lax.fori_loop(0, td, issue, 0, unroll=8)

    def drain(r, c):
        _row_copy(h_ref, 0, xb_ref, 0, sem_ref).wait()
        return c

    lax.fori_loop(0, TOP_K * td, drain, 0, unroll=8)


def _dispatch(dest2, h2, cap):
    t, d = h2.shape
    n_steps = dest2.shape[0]
    anyspec = pl.BlockSpec(memory_space=pl.ANY)
    return pl.pallas_call(
        _dispatch_kernel,
        out_shape=jax.ShapeDtypeStruct((cap, d), h2.dtype),
        grid=(n_steps,),
        in_specs=[pl.BlockSpec((None, 1, dest2.shape[2]), lambda i: (i, 0, 0), memory_space=pltpu.SMEM),
                  pl.BlockSpec((dest2.shape[2] // TOP_K, d), lambda i: (i, 0)), anyspec],
        out_specs=anyspec,
        scratch_shapes=[pltpu.SemaphoreType.DMA((1,))],
        input_output_aliases={2: 0},
        compiler_params=_cparams(("arbitrary",)),
        name="moe_dispatch",
    )(dest2, h2, jnp.zeros((cap, d), h2.dtype))


def _combine_kernel(dest_ref, yb_ref, xn_ref, mod_ref, gate_ref, g_ref, o_ref, buf_ref, sem_ref, *, final):
    tc = xn_ref.shape[0]

    def issue(r, c):
        for k in range(TOP_K):
            pltpu.make_async_copy(yb_ref.at[pl.ds(dest_ref[0, TOP_K * r + k], 1)], buf_ref.at[k, pl.ds(r, 1)],
                                  sem_ref.at[0]).start(priority=k % 2)
        return c

    lax.fori_loop(0, tc, issue, 0, unroll=8)

    def drain(r, c):
        pltpu.make_async_copy(yb_ref.at[pl.ds(0, 1)], buf_ref.at[0, pl.ds(0, 1)], sem_ref.at[0]).wait()
        return c

    lax.fori_loop(0, TOP_K * tc, drain, 0, unroll=8)
    gate = gate_ref[...].astype(BF16).astype(F32)
    y = buf_ref[0].astype(BF16).astype(F32) * gate[:, 0:1]
    for k in range(1, TOP_K):
        y = y + buf_ref[k].astype(BF16).astype(F32) * gate[:, k:k + 1]
    x = xn_ref[...] + mod_ref[5:6, :] * y
    o_ref[...] = _rms(x, g_ref[...]) if final else x


def _combine(dest2, yb, xn, mod, gate, g_final, tc, final):
    b, l, d = xn.shape
    nt = l // tc
    anyspec = pl.BlockSpec(memory_space=pl.ANY)
    return pl.pallas_call(
        functools.partial(_combine_kernel, final=final),
        out_shape=jax.ShapeDtypeStruct((b, l, d), F32),
        grid=(b, nt),
        in_specs=[pl.BlockSpec((None, 1, TOP_K * tc), lambda bi, t: (bi * nt + t, 0, 0), memory_space=pltpu.SMEM),
                  anyspec,
                  pl.BlockSpec((None, tc, d), lambda bi, t: (bi, t, 0)),
                  pl.BlockSpec((None, 6, d), lambda bi, t: (bi, 0, 0)),
                  pl.BlockSpec((None, tc, TOP_K), lambda bi, t: (bi, t, 0)),
                  pl.BlockSpec((1, d), lambda bi, t: (0, 0))],
        out_specs=pl.BlockSpec((None, tc, d), lambda bi, t: (bi, t, 0)),
        scratch_shapes=[pltpu.VMEM((TOP_K, tc, d), F32), pltpu.SemaphoreType.DMA((1,))],
        compiler_params=_cparams(("arbitrary", "arbitrary")),
        name="moe_combine",
    )(dest2, yb, xn, mod, gate.reshape(b, l, TOP_K), g_final)


def _moe(h2, lgt, router_bias, w1_b, w3_b, w2_b, tr):
    t, d = h2.shape
    n_experts = w1_b.shape[0]
    idx, rank, gate, counts = _route(lgt, router_bias, tr)
    s = t * TOP_K
    padded = (counts + MOE_ROWS - 1) // MOE_ROWS * MOE_ROWS
    pad_end = jnp.cumsum(padded)
    pad_start = pad_end - padded
    onehot = idx[..., None] == jnp.arange(n_experts)
    dest = rank + jnp.sum(jnp.where(onehot, pad_start, 0), axis=-1)
    n_blocks = -(-s // MOE_ROWS) + n_experts
    cap = n_blocks * MOE_ROWS
    dest2 = dest.reshape(t // MOE_TOK, 1, MOE_TOK * TOP_K).astype(jnp.int32)
    xb = _dispatch(dest2, h2, cap)
    block_start = jnp.arange(n_blocks, dtype=pad_end.dtype) * MOE_ROWS
    block_e = jnp.minimum(jnp.sum(pad_end[None, :] <= block_start[:, None], axis=1), n_experts - 1).astype(jnp.int32)
    n_active = (pad_end[-1] // MOE_ROWS).astype(jnp.int32).reshape(1)
    yb = _moe_blocks(block_e, n_active, xb, w1_b, w3_b, w2_b)
    return yb, dest2, gate


def _final_kernel(x_ref, g_ref, o_ref):
    o_ref[...] = _rms(x_ref[...], g_ref[...])


def _final_norm(x2d, g):
    t, d = x2d.shape
    tm = min(t, 512)
    return pl.pallas_call(
        _final_kernel,
        out_shape=jax.ShapeDtypeStruct((t, d), F32),
        grid=(t // tm,),
        in_specs=[pl.BlockSpec((tm, d), lambda i: (i, 0)), pl.BlockSpec((1, d), lambda i: (0, 0))],
        out_specs=pl.BlockSpec((tm, d), lambda i: (i, 0)),
        compiler_params=_cparams(("arbitrary",)),
        name="final_norm",
    )(x2d, g)


def _alibi_slopes(n_heads):
    return 2.0 ** (-8.0 * (jnp.arange(n_heads, dtype=F32) + 1.0) / n_heads)


def _spre_kernel(x_ref, sh_ref, sc_ref, g_ref, w_ref, o_ref):
    h = _rms(x_ref[...], g_ref[...]) * (1.0 + sc_ref[...]) + sh_ref[...]
    o_ref[...] = jnp.dot(h.astype(BF16), w_ref[...], preferred_element_type=F32)


def _sample_pre(x, sh, sc, g, w_in_b):
    db, d = x.shape
    n = w_in_b.shape[1]
    full = lambda shape: pl.BlockSpec(shape, lambda i: (0,) * len(shape))
    return pl.pallas_call(
        _spre_kernel,
        out_shape=jax.ShapeDtypeStruct((db, n), F32),
        grid=(1,),
        in_specs=[full((db, d)), full((db, d)), full((db, d)), full((1, d)), full((d, n))],
        out_specs=full((db, n)),
        compiler_params=_cparams(("arbitrary",)),
        name="sample_pre",
    )(x, sh, sc, g, w_in_b)


def _sgate_kernel(q_ref, km_ref, o_ref):
    a, n_fp = km_ref.shape
    n_heads = a // HEAD_DIM
    hrow = lax.broadcasted_iota(jnp.int32, (n_heads, a), 0)
    lane = lax.broadcasted_iota(jnp.int32, (n_heads, a), 1)
    qrows = jnp.where(lane // HEAD_DIM == hrow, jnp.broadcast_to(q_ref[...], (n_heads, a)), 0.0).astype(BF16)
    gate = jnp.dot(qrows, km_ref[...].astype(BF16), preferred_element_type=F32)
    jl = lax.broadcasted_iota(jnp.int32, gate.shape, 1)
    ol = lax.broadcasted_iota(jnp.int32, o_ref.shape, 1)
    out = jnp.zeros(o_ref.shape, jnp.int32)
    for r in range(MOBA_TOPK):
        mx = jnp.max(gate, axis=1, keepdims=True)
        idx = jnp.min(jnp.where(gate == mx, jl, n_fp), axis=1, keepdims=True)
        out = jnp.where(ol == r, idx, out)
        gate = jnp.where(jl == idx, -jnp.inf, gate)
    o_ref[...] = out


def _sample_gate(q3, kmean_l):
    db, _, a = q3.shape
    n_fp = kmean_l.shape[2]
    n_heads = a // HEAD_DIM
    return pl.pallas_call(
        _sgate_kernel,
        out_shape=jax.ShapeDtypeStruct((db, n_heads, LANES), jnp.int32),
        grid=(db,),
        in_specs=[pl.BlockSpec((None, 1, a), lambda b: (b, 0, 0)),
                  pl.BlockSpec((None, a, n_fp), lambda b: (b, 0, 0))],
        out_specs=pl.BlockSpec((None, n_heads, LANES), lambda b: (b, 0, 0)),
        compiler_params=_cparams(("arbitrary",)),
        name="sample_gate",
    )(q3, kmean_l)


def _sattn_copies(pt_ref, sel_ref, ck_ref, cv_ref, kb_ref, vb_ref, sem_ref, layer, slot, b, n_heads, ppb):
    out = []
    for h in range(n_heads):
        for r in range(MOBA_TOPK):
            blk = sel_ref[b, h * MOBA_TOPK + r]
            for pg in range(ppb):
                phys = pt_ref[b, blk * ppb + pg]
                i = r * ppb + pg
                out.append(pltpu.make_async_copy(ck_ref.at[layer, phys, h], kb_ref.at[slot, h, i], sem_ref.at[slot]))
                out.append(pltpu.make_async_copy(cv_ref.at[layer, phys, h], vb_ref.at[slot, h, i], sem_ref.at[slot]))
    return out


def _sattn_kernel(pt_ref, sel_ref, q_ref, k_ref, v_ref, ck_ref, cv_ref, o_ref, kb_ref, vb_ref, sem_ref, *, layer, past_len):
    b = pl.program_id(0)
    n_b = pl.num_programs(0)
    slot = b % 2
    n_heads, n_pg, hd, page = kb_ref.shape[1:]
    ppb = MOBA_BLOCK // page
    args = (pt_ref, sel_ref, ck_ref, cv_ref, kb_ref, vb_ref, sem_ref, layer)

    @pl.when(b == 0)
    def _():
        for c in _sattn_copies(*args, 0, b, n_heads, ppb):
            c.start()

    @pl.when(b + 1 < n_b)
    def _():
        for c in _sattn_copies(*args, 1 - slot, b + 1, n_heads, ppb):
            c.start()

    for c in _sattn_copies(*args, slot, b, n_heads, ppb):
        c.wait()

    n_sel = n_pg * page
    kpos = lax.broadcasted_iota(jnp.int32, (1, n_sel), 1)
    outs = []
    for h in range(n_heads):
        hs = slice(h * hd, (h + 1) * hd)
        slope = 2.0 ** (-8.0 * (h + 1.0) / n_heads)
        qf = (q_ref[:, hs] * HEAD_DIM ** -0.5).astype(BF16)
        q8 = jnp.broadcast_to(qf, (8, hd))
        l_sel = jnp.concatenate([jnp.dot(q8, kb_ref[slot, h, i].astype(BF16), preferred_element_type=F32)
                                 for i in range(n_pg)], axis=1)[0:1]
        blk_of = jnp.zeros((1, n_sel), jnp.int32)
        for r in range(MOBA_TOPK):
            blk_of = jnp.where(kpos // MOBA_BLOCK == r, sel_ref[b, h * MOBA_TOPK + r], blk_of)
        pos_sel = blk_of * MOBA_BLOCK + kpos % MOBA_BLOCK
        l_sel = l_sel - slope * (past_len - pos_sel).astype(F32)
        kown = k_ref[:, hs].astype(BF16).astype(F32)
        vown = v_ref[:, hs].astype(BF16).astype(F32)
        l_own = jnp.sum(qf.astype(F32) * kown, axis=1, keepdims=True)
        mx = jnp.maximum(jnp.max(l_sel, axis=1, keepdims=True), l_own)
        e_sel = jnp.exp(l_sel - mx)
        e_own = jnp.exp(l_own - mx)
        den = jnp.sum(e_sel, axis=1, keepdims=True) + e_own
        p8 = jnp.broadcast_to((e_sel / den).astype(BF16), (8, n_sel))
        p_own = (e_own / den).astype(BF16).astype(F32)
        o_h = p_own * vown
        for i in range(n_pg):
            o_h = o_h + lax.dot_general(p8[:, i * page:(i + 1) * page], vb_ref[slot, h, i].astype(BF16), _NT,
                                        preferred_element_type=F32)[0:1]
        outs.append(o_h)
    o_ref[...] = jnp.concatenate(outs, axis=1)


def _sample_attn(page_table, sel, q3, k3, v3, cache_kt, cache_vt, layer, past_len):
    db, _, a = q3.shape
    n_heads, hd, page = cache_kt.shape[2:]
    ppb = MOBA_BLOCK // page
    row = pl.BlockSpec((None, 1, a), lambda b, pt, sl: (b, 0, 0))
    anyspec = pl.BlockSpec(memory_space=pl.ANY)
    grid_spec = pltpu.PrefetchScalarGridSpec(
        num_scalar_prefetch=2,
        grid=(db,),
        in_specs=[row, row, row, anyspec, anyspec],
        out_specs=row,
        scratch_shapes=[pltpu.VMEM((2, n_heads, MOBA_TOPK * ppb, hd, page), F32),
                        pltpu.VMEM((2, n_heads, MOBA_TOPK * ppb, hd, page), F32),
                        pltpu.SemaphoreType.DMA((2,))])
    return pl.pallas_call(
        functools.partial(_sattn_kernel, layer=layer, past_len=past_len),
        out_shape=jax.ShapeDtypeStruct((db, 1, a), F32),
        grid_spec=grid_spec,
        compiler_params=_cparams(("arbitrary",)),
        name="sample_attn",
    )(page_table, sel, q3, k3, v3, cache_kt, cache_vt)


def _spost_kernel(x_ref, attn_ref, u_ref, st_ref, ga_ref, sh_ref, sc_ref, wp_ref, ps_ref, wo_ref, g_ref, wr_ref,
                  xo_ref, h2_ref, lg_ref, *, past_len):
    a = attn_ref.shape[-1]
    u = u_ref[...]
    st = st_ref[...]
    n_st = st.shape[1]
    gw = a // len(POOL_WINDOWS)
    srow = lax.broadcasted_iota(jnp.int32, (1, n_st, a), 1)
    lane = lax.broadcasted_iota(jnp.int32, (1, n_st, a), 2)
    win = jnp.zeros((1, n_st, a), jnp.int32)
    cnt = jnp.zeros((1, a), F32)
    lane2 = lax.broadcasted_iota(jnp.int32, (1, a), 1)
    for g, w in enumerate(POOL_WINDOWS):
        win = jnp.where(lane // gw == g, w, win)
        cnt = jnp.where(lane2 // gw == g, float(min(w, past_len + 1)), cnt)
    wsum = jnp.sum(jnp.where(srow >= n_st + 1 - win, st, 0.0), axis=1) + u
    diff = (wsum / cnt - u).astype(BF16)
    pools = [jnp.dot(diff[:, g * gw:(g + 1) * gw], wp_ref[g], preferred_element_type=F32)
             for g in range(len(POOL_WINDOWS))]
    pool = jnp.concatenate(pools, axis=1) * ps_ref[...]
    mo = (jnp.dot(attn_ref[...].astype(BF16), wo_ref[0:a, :], preferred_element_type=F32)
          + jnp.dot(pool.astype(BF16), wo_ref[a:, :], preferred_element_type=F32))
    xn = x_ref[...] + ga_ref[...] * mo
    xo_ref[...] = xn
    hb = (_rms(xn, g_ref[...]) * (1.0 + sc_ref[...]) + sh_ref[...]).astype(BF16)
    h2_ref[...] = hb
    lg_ref[...] = lax.dot_general(wr_ref[...], hb, _NT, preferred_element_type=F32)


def _sample_post(x, attn, u, state, ga1, sh2, sc2, w_pool_b, pool_scale, w_out_b, g_ffn, wr_b, past_len):
    db, d = x.shape
    ne = wr_b.shape[0]
    ins = (x, attn, u, state, ga1, sh2, sc2, w_pool_b, pool_scale, w_out_b, g_ffn, wr_b)
    full = lambda arr: pl.BlockSpec(arr.shape, lambda i, n=arr.ndim: (0,) * n)
    outs = (jax.ShapeDtypeStruct((db, d), F32), jax.ShapeDtypeStruct((db, d), BF16), jax.ShapeDtypeStruct((ne, db), F32))
    return pl.pallas_call(
        functools.partial(_spost_kernel, past_len=past_len),
        out_shape=outs,
        grid=(1,),
        in_specs=[full(x_) for x_ in ins],
        out_specs=tuple(full(o) for o in outs),
        compiler_params=_cparams(("arbitrary",)),
        name="sample_post",
    )(*ins)


def _smoe_kernel(h_ref, cb_ref, w1_ref, w3_ref, w2_ref, xn_ref, ga_ref, o_ref, acc_ref):
    e = pl.program_id(0)

    @pl.when(e == 0)
    def _():
        acc_ref[...] = jnp.zeros_like(acc_ref)

    h = h_ref[...]
    a = jnp.dot(h, w1_ref[...], preferred_element_type=F32)
    g = jnp.dot(h, w3_ref[...], preferred_element_type=F32)
    y = jnp.dot(((a * jax.nn.sigmoid(a)) * g).astype(BF16), w2_ref[...], preferred_element_type=F32)
    acc_ref[...] += y.astype(BF16).astype(F32) * cb_ref[...].astype(BF16).astype(F32)

    @pl.when(e == pl.num_programs(0) - 1)
    def _():
        o_ref[...] = xn_ref[...] + ga_ref[...] * acc_ref[...]


def _sample_moe(h2b, comb, w1_b, w3_b, w2_b, xn, ga2):
    db, d = h2b.shape
    n_e, _, de = w1_b.shape
    cst = lambda e: (0, 0)
    return pl.pallas_call(
        _smoe_kernel,
        out_shape=jax.ShapeDtypeStruct((db, d), F32),
        grid=(n_e,),
        in_specs=[pl.BlockSpec((db, d), cst),
                  pl.BlockSpec((None, db, 1), lambda e: (e, 0, 0)),
                  pl.BlockSpec((None, d, de), lambda e: (e, 0, 0)),
                  pl.BlockSpec((None, d, de), lambda e: (e, 0, 0)),
                  pl.BlockSpec((None, de, d), lambda e: (e, 0, 0)),
                  pl.BlockSpec((db, d), cst),
                  pl.BlockSpec((db, d), cst)],
        out_specs=pl.BlockSpec((db, d), cst),
        scratch_shapes=[pltpu.VMEM((db, d), F32)],
        compiler_params=_cparams(("arbitrary",)),
        name="sample_moe",
    )(h2b, comb, w1_b, w3_b, w2_b, xn, ga2)


def _sample_layer(x, mod_s, layer, kmean_l, cache_kt, cache_vt, page_table, state_l, g_mix_l, w_in_b_l, w_pool_b_l,
                  pool_scale_l, w_out_b_l, g_ffn_l, wr_b, router_bias, w1_b_l, w3_b_l, w2_b_l, past_len):
    db, d = x.shape
    a = kmean_l.shape[1]
    n_e = w1_b_l.shape[0]
    sh1, sc1, ga1, sh2, sc2, ga2 = [mod_s[:, i] for i in range(6)]
    r = _sample_pre(x, sh1, sc1, g_mix_l, w_in_b_l)
    q3, k3, v3, u = r[:, None, :a], r[:, None, a:2 * a], r[:, None, 2 * a:3 * a], r[:, 3 * a:]
    sel = _sample_gate(q3, kmean_l)[:, :, :MOBA_TOPK].reshape(db, -1)
    attn = _sample_attn(page_table, sel, q3, k3, v3, cache_kt, cache_vt, layer, past_len)[:, 0]
    xn, h2b, lgt = _sample_post(x, attn, u, state_l, ga1, sh2, sc2, w_pool_b_l, pool_scale_l, w_out_b_l,
                                g_ffn_l, wr_b, past_len)
    idx, _, gate, _ = _route(lgt[None], router_bias, db)
    comb = jnp.sum(jnp.where(idx[None, :, :] == jnp.arange(n_e)[:, None, None], gate[None], 0.0), axis=-1)
    x_out = _sample_moe(h2b, comb[..., None], w1_b_l, w3_b_l, w2_b_l, xn, ga2)
    return x_out, k3, v3, u


def kernel(x_prompt, x_sample, cache_k, cache_v, state_pool, page_table, c_prompt, c_sample,
           w_ada, b_ada, g_mix, w_in, w_pool, pool_scale, w_out, g_ffn, w_router, router_bias,
           w1, w3, w2, g_final):
    b, l, d = x_prompt.shape
    db, ds, _ = x_sample.shape
    depth = w_ada.shape[0]
    n_heads = cache_k.shape[3]
    a = n_heads * HEAD_DIM
    n_experts = w1.shape[1]
    tm = 512
    assert l % tm == 0 and tm % MOBA_BLOCK == 0 and a % LANES == 0 and n_experts <= 16

    mod = _ada(jnp.concatenate([c_prompt, c_sample], axis=0), w_ada, b_ada)
    mod = mod.reshape(depth, b + db, 6, d)
    slopes = _alibi_slopes(n_heads)

    w_q, w_k, w_v = w_in[:, :, :a], w_in[:, :, a:2 * a], w_in[:, :, 2 * a:3 * a]

    def head_pad(w):
        w = w.reshape(depth, d, n_heads, HEAD_DIM)
        return jnp.pad(w, ((0, 0), (0, 0), (0, 0), (0, HEAD_PAD - HEAD_DIM))).reshape(depth, d, n_heads * HEAD_PAD)

    wn_b = w_in[:, :, a:].astype(BF16)
    wa_b = jnp.concatenate([head_pad(w_q), head_pad(w_k)], axis=-1).astype(BF16)
    wvt_b = jnp.swapaxes(w_v, 1, 2).astype(BF16)
    slopes_np = 2.0 ** (-8.0 * (np.arange(n_heads, dtype=np.float64) + 1.0) / n_heads)
    key_term = (slopes_np[:, None] * np.arange(MOBA_BLOCK)[None, :]).astype(np.float32)
    assert np.array_equal(key_term.astype(BF16).astype(np.float32), key_term), \
        "ALiBi key term must be exact in bfloat16 to ride in the padded key lanes"
    alibi_lane = np.zeros((n_heads, HEAD_PAD), np.float32)
    alibi_lane[:, HEAD_DIM] = 1.0
    orow = jnp.asarray(alibi_lane.reshape(1, -1))
    srow = jnp.asarray((alibi_lane * slopes_np[:, None].astype(np.float32)).reshape(1, -1))
    w_out_b = w_out.astype(BF16)
    w_pool_b = w_pool.astype(BF16)
    w1_b, w3_b, w2_b = w1.astype(BF16), w3.astype(BF16), w2.astype(BF16)
    wr_b = w_router.T.astype(BF16)

    w_in_b = w_in.astype(BF16)
    page = cache_k.shape[2]
    n_pool = cache_k.shape[1]
    past_len = page_table.shape[1] * page
    n_fp = past_len // MOBA_BLOCK
    assert ds == 1 and past_len % MOBA_BLOCK == 0 and n_fp >= MOBA_TOPK and MOBA_BLOCK % page == 0
    cache_kt = jnp.transpose(cache_k, (0, 1, 3, 4, 2))
    cache_vt = jnp.transpose(cache_v, (0, 1, 3, 4, 2))
    cache_kt4 = cache_kt.reshape(depth, n_pool, a, page)

    xp, xs = x_prompt, x_sample.reshape(db, d)
    kp, vp, up, kq, vq, uq = [], [], [], [], [], []
    for li in range(depth):
        mod_p = mod[li, :b]
        k, v, u, qa, ka, vt, kmean = _qkvu(xp, mod_p, g_mix[li][None], wn_b[li], wa_b[li], wvt_b[li], srow, orow, tm)
        kp.append(k.reshape(b, l, n_heads, HEAD_DIM))
        vp.append(v.reshape(b, l, n_heads, HEAD_DIM))
        attn, kmean_s = _moba_prompt(slopes, page_table, qa, ka, vt, kmean, cache_kt4, li, n_fp)
        xn, h2, lg = _mix(xp, attn, u, mod_p, w_pool_b[li], pool_scale[li][None], w_out_b[li],
                          g_ffn[li][None], wr_b, tm)
        yb, dest2, gate = _moe(h2.reshape(b * l, d), lg, router_bias, w1_b[li], w3_b[li], w2_b[li], tm)
        xp = _combine(dest2, yb, xn, mod_p, gate, g_final[None], MOE_TOK, li == depth - 1)
        up.append(u[:, l - POOL_STATE_LEN:])

        xs, k3, v3, u_s = _sample_layer(
            xs, mod[li, b:], li, kmean_s, cache_kt, cache_vt, page_table, state_pool[li], g_mix[li][None],
            w_in_b[li], w_pool_b[li], pool_scale[li][None], w_out_b[li], g_ffn[li][None], wr_b, router_bias,
            w1_b[li], w3_b[li], w2_b[li], past_len)
        kq.append(k3.reshape(db, ds, n_heads, HEAD_DIM))
        vq.append(v3.reshape(db, ds, n_heads, HEAD_DIM))
        uq.append(jnp.concatenate([state_pool[li][:, ds:], u_s[:, None]], axis=1))

    y_prompt = xp
    y_sample = _final_norm(xs, g_final[None]).reshape(db, ds, d)
    return (y_prompt, y_sample, jnp.stack(kp), jnp.stack(vp), jnp.stack(up),
            jnp.stack(kq), jnp.stack(vq), jnp.stack(uq))
```

```python
import functools

import numpy as np
import jax
import jax.numpy as jnp
from jax import lax
from jax.experimental import pallas as pl
from jax.experimental.pallas import tpu as pltpu

F32 = jnp.float32
BF16 = jnp.bfloat16

HEAD_DIM = 64
MOBA_BLOCK = 256
MOBA_TOPK = 3
POOL_WINDOWS = (2, 4, 8, 16)
POOL_STATE_LEN = max(POOL_WINDOWS) - 1
POOL_HALO = 16
N_EXPERT_GROUPS = 4
TOP_K = 2
NORM_EPS = 1e-6
LANES = 128
HEAD_PAD = 128
ATTN_GROUP = 4
ATTN_HEADS = 4
KMEAN_PAGES = 16
MOE_TOK = 512
MOE_ROWS = 256
VMEM_LIMIT = 48 * 1024 * 1024

_NT = (((1,), (1,)), ((), ()))
_TN = (((0,), (0,)), ((), ()))


def _cparams(sem):
    return pltpu.CompilerParams(dimension_semantics=sem, vmem_limit_bytes=VMEM_LIMIT)


def _rms(x, g):
    return x * lax.rsqrt(jnp.mean(x * x, axis=-1, keepdims=True) + NORM_EPS) * g


def _ada_kernel(c_ref, w_ref, b_ref, o_ref):
    c = c_ref[...]
    s = c * jax.nn.sigmoid(c)
    o_ref[...] = jnp.dot(s.astype(BF16), w_ref[...].astype(BF16), preferred_element_type=F32) + b_ref[...]


def _ada(c, w_ada, b_ada):
    depth, d, n = w_ada.shape
    nb = c.shape[0]
    tn = 1536
    return pl.pallas_call(
        _ada_kernel,
        out_shape=jax.ShapeDtypeStruct((depth, nb, n), F32),
        grid=(depth, n // tn),
        in_specs=[pl.BlockSpec((nb, d), lambda l, j: (0, 0)),
                  pl.BlockSpec((None, d, tn), lambda l, j: (l, 0, j)),
                  pl.BlockSpec((None, 1, tn), lambda l, j: (l, 0, j))],
        out_specs=pl.BlockSpec((None, nb, tn), lambda l, j: (l, 0, j)),
        compiler_params=_cparams(("arbitrary", "arbitrary")),
        name="ada",
    )(c, w_ada, b_ada.reshape(depth, 1, n))


def _qkvu_kernel(x_ref, mod_ref, g_ref, wn_ref, wa_ref, wvt_ref, srow_ref, orow_ref,
                 k_ref, v_ref, u_ref, qa_ref, ka_ref, vt_ref, km_ref):
    a = k_ref.shape[-1]
    tm = x_ref.shape[0]
    ap = qa_ref.shape[-1]
    h = _rms(x_ref[...], g_ref[...]) * (1.0 + mod_ref[1:2, :]) + mod_ref[0:1, :]
    hb = h.astype(BF16)
    rn = jnp.dot(hb, wn_ref[...], preferred_element_type=F32)
    k_ref[...] = rn[:, :a]
    v_ref[...] = rn[:, a:2 * a]
    u_ref[...] = rn[:, 2 * a:]
    ra = jnp.dot(hb, wa_ref[...], preferred_element_type=F32)
    qa_ref[...] = (ra[:, :ap] * HEAD_DIM ** -0.5 + orow_ref[...]).astype(BF16)
    kp = ra[:, ap:]
    ridx = (lax.broadcasted_iota(jnp.int32, (tm, 1), 0) % MOBA_BLOCK).astype(F32)
    ka_ref[...] = (kp + ridx * srow_ref[...]).astype(BF16)
    vt = lax.dot_general(wvt_ref[...], hb, _NT, preferred_element_type=F32)
    for s in range(km_ref.shape[0]):
        rows = slice(s * MOBA_BLOCK, (s + 1) * MOBA_BLOCK)
        km_ref[s:s + 1, :] = jnp.mean(kp[rows], axis=0, keepdims=True)
        vt_ref[s] = vt[:, rows].astype(BF16)


def _qkvu(x, mod, g, wn_b, wa_b, wvt_b, srow, orow, tm):
    b, l, d = x.shape
    a = wn_b.shape[1] // 3
    ap = wa_b.shape[1] // 2
    nkb_t = tm // MOBA_BLOCK
    tok = lambda bi, t: (bi, t, 0)
    cst = lambda bi, t: (0, 0)
    outs = pl.pallas_call(
        _qkvu_kernel,
        out_shape=(jax.ShapeDtypeStruct((b, l, a), F32),
                   jax.ShapeDtypeStruct((b, l, a), F32),
                   jax.ShapeDtypeStruct((b, l, a), F32),
                   jax.ShapeDtypeStruct((b, l, ap), BF16),
                   jax.ShapeDtypeStruct((b, l, ap), BF16),
                   jax.ShapeDtypeStruct((b, l // MOBA_BLOCK, a, MOBA_BLOCK), BF16),
                   jax.ShapeDtypeStruct((b, l // tm, nkb_t, ap), F32)),
        grid=(b, l // tm),
        in_specs=[pl.BlockSpec((None, tm, d), tok),
                  pl.BlockSpec((None, 6, d), lambda bi, t: (bi, 0, 0)),
                  pl.BlockSpec((1, d), cst),
                  pl.BlockSpec((d, 3 * a), cst),
                  pl.BlockSpec((d, 2 * ap), cst),
                  pl.BlockSpec((a, d), cst),
                  pl.BlockSpec((1, ap), cst),
                  pl.BlockSpec((1, ap), cst)],
        out_specs=(pl.BlockSpec((None, tm, a), tok),) * 3 + (pl.BlockSpec((None, tm, ap), tok),) * 2
        + (pl.BlockSpec((None, nkb_t, a, MOBA_BLOCK), lambda bi, t: (bi, t, 0, 0)),
           pl.BlockSpec((None, None, nkb_t, ap), lambda bi, t: (bi, t, 0, 0))),
        compiler_params=_cparams(("arbitrary", "arbitrary")),
        name="qkvu",
    )(x, mod, g, wn_b, wa_b, wvt_b, srow, orow)
    k, v, u, qa, ka, vt, km = outs
    return k, v, u, qa, ka, vt, km.reshape(b, l // MOBA_BLOCK, ap)


def _cache_unit_copies(pt_ref, cache_ref, buf_ref, sem_ref, layer, slot, unit, units_per_row):
    row, t = unit // units_per_row, unit % units_per_row
    return [pltpu.make_async_copy(cache_ref.at[layer, pt_ref[row, t * KMEAN_PAGES + pg]], buf_ref.at[slot, pg],
                                  sem_ref.at[slot]) for pg in range(KMEAN_PAGES)]


def _attn_kernel(sl_ref, pt_ref, q_ref, k_ref, vt_ref, km_ref, cache_ref, o_ref, kmo_ref, seln_ref, cbuf_ref, csem_ref,
                 *, layer, n_units, units_per_row):
    p = pl.program_id(1)
    i = pl.program_id(2)
    blk = MOBA_BLOCK
    n_kb = km_ref.shape[0]
    nh = ATTN_HEADS

    step = (pl.program_id(0) * pl.num_programs(1) + p) * pl.num_programs(2) + i
    n_steps = pl.num_programs(0) * pl.num_programs(1) * pl.num_programs(2)
    slot = step % 2
    unit = jnp.minimum(step, n_units - 1)
    cargs = (pt_ref, cache_ref, cbuf_ref, csem_ref, layer)

    @pl.when(step == 0)
    def _():
        for c in _cache_unit_copies(*cargs, 0, unit, units_per_row):
            c.start()

    @pl.when(step + 1 < n_steps)
    def _():
        for c in _cache_unit_copies(*cargs, 1 - slot, jnp.minimum(step + 1, n_units - 1), units_per_row):
            c.start()

    t = unit % units_per_row

    @pl.when(t == 0)
    def _():
        kmo_ref[...] = jnp.zeros_like(kmo_ref)

    for c in _cache_unit_copies(*cargs, slot, unit, units_per_row):
        c.wait()
    ppb = MOBA_BLOCK // cbuf_ref.shape[3]
    gblk = KMEAN_PAGES // ppb
    klane = lax.broadcasted_iota(jnp.int32, kmo_ref.shape, 1)
    kacc = kmo_ref[...]
    for g in range(gblk):
        tot = cbuf_ref[slot, g * ppb]
        for q in range(1, ppb):
            tot = tot + cbuf_ref[slot, g * ppb + q]
        col = jnp.sum(tot, axis=1, keepdims=True) * (1.0 / MOBA_BLOCK)
        kacc = jnp.where(klane == t * gblk + g, col, kacc)
    kmo_ref[...] = kacc

    kidx = lax.broadcasted_iota(jnp.int32, (blk, blk), 0)
    qidx = lax.broadcasted_iota(jnp.int32, (blk, blk), 1)
    causal = kidx <= qidx
    jrow = lax.broadcasted_iota(jnp.int32, (n_kb, blk), 0)
    start = pl.multiple_of(i * blk, blk)
    hls = [slice(s * HEAD_PAD, (s + 1) * HEAD_PAD) for s in range(nh)]
    hrs = [slice(s * HEAD_DIM, (s + 1) * HEAD_DIM) for s in range(nh)]
    slopes = [sl_ref[nh * p + s] for s in range(nh)]
    init = []
    for s in range(nh):
        qh = q_ref[:, hls[s]]
        gate = lax.dot_general(km_ref[:, hls[s]].astype(BF16), qh, _NT, preferred_element_type=F32)
        gate = jnp.where(jrow < i, gate, -jnp.inf)
        sel = jnp.zeros(gate.shape, jnp.bool_)
        for r in range(MOBA_TOPK):
            mx = jnp.max(gate, axis=0, keepdims=True)
            idx = jnp.min(jnp.where(gate == mx, jrow, n_kb), axis=0, keepdims=True)
            pick = jrow == idx
            sel = sel | (pick & (i > r))
            gate = jnp.where(pick, -jnp.inf, gate)
        seln_ref[s] = jnp.where(sel, 0.0, -jnp.inf)
        st = lax.dot_general(k_ref[pl.ds(start, blk), hls[s]], qh, _NT, preferred_element_type=F32)
        st = jnp.where(causal, st, -jnp.inf)
        m0 = jnp.max(st, axis=0, keepdims=True)
        pt = jnp.exp(st - m0)
        l0 = jnp.sum(pt, axis=0, keepdims=True)
        acc0 = jnp.dot(vt_ref[i, hrs[s], :], pt.astype(BF16), preferred_element_type=F32)
        init.append((m0, l0, acc0))

    def scores(g, s, m_old):
        qh = q_ref[:, hls[s]]
        sts, offs = [], []
        m_new = m_old
        for u in range(ATTN_GROUP):
            j = g * ATTN_GROUP + u
            ks = pl.multiple_of(j * blk, blk)
            st_u = lax.dot_general(k_ref[pl.ds(ks, blk), hls[s]], qh, _NT, preferred_element_type=F32)
            cj = slopes[s] * ((j - i) * blk).astype(F32)
            off = cj + seln_ref[s, pl.ds(j, 1), :]
            m_new = jnp.maximum(m_new, jnp.max(st_u, axis=0, keepdims=True) + off)
            sts.append(st_u)
            offs.append(off)
        return sts, offs, m_new

    def accumulate(g, s, state, staged):
        m_old, l_old, acc_old = state
        sts, offs, m_new = staged
        alpha = jnp.exp(m_old - m_new)
        l_new = alpha * l_old
        acc_new = alpha * acc_old
        for u in range(ATTN_GROUP):
            j = g * ATTN_GROUP + u
            pt_u = jnp.exp(sts[u] - (m_new - offs[u]))
            l_new = l_new + jnp.sum(pt_u, axis=0, keepdims=True)
            acc_new = acc_new + jnp.dot(vt_ref[j, hrs[s], :], pt_u.astype(BF16), preferred_element_type=F32)
        return m_new, l_new, acc_new

    def group(g, carry):
        staged = scores(g, 0, carry[0][0])
        out = []
        for s in range(nh):
            nxt = scores(g, s + 1, carry[s + 1][0]) if s + 1 < nh else None
            out.append(accumulate(g, s, carry[s], staged))
            staged = nxt
        return tuple(out)

    final = lax.fori_loop(0, (i + ATTN_GROUP - 1) // ATTN_GROUP, group, tuple(init))
    ot = jnp.concatenate([acc_f / l_f for (_, l_f, acc_f) in final], axis=0)
    o_ref[...] = ot.T.astype(o_ref.dtype)


def _moba_prompt(slopes, page_table, qa, ka, vt, kmean, cache_kt, layer, n_fp):
    b, l, ap = qa.shape
    n_kb = l // MOBA_BLOCK
    blk = MOBA_BLOCK
    nh = ATTN_HEADS
    n_hg = ap // (nh * HEAD_PAD)
    _, _, a, page = cache_kt.shape
    db = page_table.shape[0]
    gblk = KMEAN_PAGES // (MOBA_BLOCK // page)
    units_per_row = n_fp // gblk
    n_units = db * units_per_row
    n_steps = b * n_hg * n_kb
    assert n_kb % ATTN_GROUP == 0 and ap % (nh * HEAD_PAD) == 0 and n_fp % gblk == 0 and n_units <= n_steps

    def kmo_index(bi, p, i, sl, pt):
        step = (bi * n_hg + p) * n_kb + i
        return (jnp.minimum(step, n_units - 1) // units_per_row, 0, 0)

    grid_spec = pltpu.PrefetchScalarGridSpec(
        num_scalar_prefetch=2,
        grid=(b, n_hg, n_kb),
        in_specs=[pl.BlockSpec((None, blk, nh * HEAD_PAD), lambda bi, p, i, sl, pt: (bi, i, p)),
                  pl.BlockSpec((None, l, nh * HEAD_PAD), lambda bi, p, i, sl, pt: (bi, 0, p)),
                  pl.BlockSpec((None, n_kb, nh * HEAD_DIM, blk), lambda bi, p, i, sl, pt: (bi, 0, p, 0)),
                  pl.BlockSpec((None, n_kb, nh * HEAD_PAD), lambda bi, p, i, sl, pt: (bi, 0, p)),
                  pl.BlockSpec(memory_space=pl.ANY)],
        out_specs=(pl.BlockSpec((None, blk, nh * HEAD_DIM), lambda bi, p, i, sl, pt: (bi, i, p)),
                   pl.BlockSpec((None, a, n_fp), kmo_index)),
        scratch_shapes=[pltpu.VMEM((nh, n_kb, blk), F32),
                        pltpu.VMEM((2, KMEAN_PAGES, a, page), F32),
                        pltpu.SemaphoreType.DMA((2,))])
    return pl.pallas_call(
        functools.partial(_attn_kernel, layer=layer, n_units=n_units, units_per_row=units_per_row),
        out_shape=(jax.ShapeDtypeStruct((b, l, n_hg * nh * HEAD_DIM), BF16),
                   jax.ShapeDtypeStruct((db, a, n_fp), F32)),
        grid_spec=grid_spec,
        compiler_params=_cparams(("arbitrary", "arbitrary", "arbitrary")),
        name="moba_prompt",
    )(slopes, page_table, qa, ka, vt, kmean, cache_kt)


def _mix_kernel(x_ref, attn_ref, u_ref, halo_ref, mod_ref, wp_ref, ps_ref, wo_ref, g_ref, wr_ref,
                xo_ref, h2_ref, lg_ref, ext_ref):
    t = pl.program_id(1)
    tm = x_ref.shape[0]
    a = attn_ref.shape[-1]
    halo = jnp.where(t == 0, 0.0, halo_ref[...])
    ext_ref[0:POOL_HALO, :] = halo
    ext_ref[POOL_HALO:POOL_HALO + tm, :] = u_ref[...]
    pos = t * tm + lax.broadcasted_iota(jnp.int32, (tm, 1), 0)
    gw = a // len(POOL_WINDOWS)
    pools = []
    for g, w in enumerate(POOL_WINDOWS):
        ln = slice(g * gw, (g + 1) * gw)
        acc = ext_ref[POOL_HALO:POOL_HALO + tm, ln]
        for s in range(1, w):
            acc = acc + ext_ref[POOL_HALO - s:POOL_HALO - s + tm, ln]
        cnt = jnp.minimum(w, pos + 1).astype(F32)
        diff = acc / cnt - u_ref[:, ln]
        pools.append(jnp.dot(diff.astype(BF16), wp_ref[g], preferred_element_type=F32))
    pool = jnp.concatenate(pools, axis=1) * ps_ref[...]
    mo = (jnp.dot(attn_ref[...], wo_ref[0:a, :], preferred_element_type=F32)
          + jnp.dot(pool.astype(BF16), wo_ref[a:, :], preferred_element_type=F32))
    xn = x_ref[...] + mod_ref[2:3, :] * mo
    xo_ref[...] = xn
    h2 = _rms(xn, g_ref[...]) * (1.0 + mod_ref[4:5, :]) + mod_ref[3:4, :]
    h2_ref[...] = h2
    lg_ref[...] = lax.dot_general(wr_ref[...], h2.astype(BF16), _NT, preferred_element_type=F32)


def _mix(x, attn, u, mod, w_pool_b, pool_scale, w_out_b, g_ffn, wr_b, tm):
    b, l, d = x.shape
    a = attn.shape[-1]
    ne = wr_b.shape[0]
    tok = lambda bi, t: (bi, t, 0)
    cst2 = lambda bi, t: (0, 0)
    hb = tm // POOL_HALO
    return pl.pallas_call(
        _mix_kernel,
        out_shape=(jax.ShapeDtypeStruct((b, l, d), F32),
                   jax.ShapeDtypeStruct((b, l, d), F32),
                   jax.ShapeDtypeStruct((b, ne, l), F32)),
        grid=(b, l // tm),
        in_specs=[pl.BlockSpec((None, tm, d), tok),
                  pl.BlockSpec((None, tm, a), tok),
                  pl.BlockSpec((None, tm, a), tok),
                  pl.BlockSpec((None, POOL_HALO, a), lambda bi, t: (bi, jnp.maximum(t * hb - 1, 0), 0)),
                  pl.BlockSpec((None, 6, d), lambda bi, t: (bi, 0, 0)),
                  pl.BlockSpec(w_pool_b.shape, lambda bi, t: (0, 0, 0)),
                  pl.BlockSpec((1, a), cst2),
                  pl.BlockSpec((d, d), cst2),
                  pl.BlockSpec((1, d), cst2),
                  pl.BlockSpec((ne, d), cst2)],
        out_specs=(pl.BlockSpec((None, tm, d), tok),
                   pl.BlockSpec((None, tm, d), tok),
                   pl.BlockSpec((None, ne, tm), lambda bi, t: (bi, 0, t))),
        scratch_shapes=[pltpu.VMEM((tm + POOL_HALO, a), F32)],
        compiler_params=_cparams(("arbitrary", "arbitrary")),
        name="mix",
    )(x, attn, u, u, mod, w_pool_b, pool_scale, w_out_b, g_ffn, wr_b)


def _moe_kernel(be_ref, na_ref, x_ref, w1_ref, w3_ref, w2_ref, o_ref):
    i = pl.program_id(0)

    @pl.when(i < na_ref[0])
    def _():
        x = x_ref[...].astype(BF16)
        a = jnp.dot(x, w1_ref[...], preferred_element_type=F32)
        g = jnp.dot(x, w3_ref[...], preferred_element_type=F32)
        hmid = (a * jax.nn.sigmoid(a)) * g
        o_ref[...] = jnp.dot(hmid.astype(BF16), w2_ref[...], preferred_element_type=F32)

    @pl.when(i >= na_ref[0])
    def _():
        o_ref[...] = jnp.zeros_like(o_ref)


def _moe_blocks(block_e, n_active, xb, w1_b, w3_b, w2_b):
    cap, d = xb.shape
    de = w1_b.shape[-1]
    n_blocks = cap // MOE_ROWS
    grid_spec = pltpu.PrefetchScalarGridSpec(
        num_scalar_prefetch=2,
        grid=(n_blocks,),
        in_specs=[pl.BlockSpec((MOE_ROWS, d), lambda i, be, na: (i, 0)),
                  pl.BlockSpec((None, d, de), lambda i, be, na: (be[i], 0, 0)),
                  pl.BlockSpec((None, d, de), lambda i, be, na: (be[i], 0, 0)),
                  pl.BlockSpec((None, de, d), lambda i, be, na: (be[i], 0, 0))],
        out_specs=pl.BlockSpec((MOE_ROWS, d), lambda i, be, na: (i, 0)))
    return pl.pallas_call(
        _moe_kernel,
        out_shape=jax.ShapeDtypeStruct((cap, d), F32),
        grid_spec=grid_spec,
        compiler_params=_cparams(("arbitrary",)),
        name="moe_blocks",
    )(block_e, n_active, xb, w1_b, w3_b, w2_b)


def _route_rows(lg, bias):
    n_e = lg.shape[0]
    epg = n_e // N_EXPERT_GROUPS
    sc = jax.nn.sigmoid(lg)
    bi = sc + bias
    s_rows = [sc[e:e + 1, :] for e in range(n_e)]
    b_rows = [bi[e:e + 1, :] for e in range(n_e)]
    gs = []
    for g in range(N_EXPERT_GROUPS):
        v = b_rows[g * epg:(g + 1) * epg]
        best = None
        for a_ in range(epg):
            for b_ in range(a_ + 1, epg):
                pr = v[a_] + v[b_]
                best = pr if best is None else jnp.maximum(best, pr)
        gs.append(best)
    grp = jnp.zeros(gs[0].shape, jnp.int32)
    gbest = gs[0]
    for g in range(1, N_EXPERT_GROUPS):
        up = gs[g] > gbest
        gbest = jnp.where(up, gs[g], gbest)
        grp = jnp.where(up, g, grp)
    c = [b_rows[j] for j in range(epg)]
    d = [s_rows[j] for j in range(epg)]
    for g in range(1, N_EXPERT_GROUPS):
        hit = grp == g
        c = [jnp.where(hit, b_rows[g * epg + j], c[j]) for j in range(epg)]
        d = [jnp.where(hit, s_rows[g * epg + j], d[j]) for j in range(epg)]

    def first_argmax(vals):
        idx = jnp.zeros(vals[0].shape, jnp.int32)
        best = vals[0]
        for j in range(1, len(vals)):
            up = vals[j] > best
            best = jnp.where(up, vals[j], best)
            idx = jnp.where(up, j, idx)
        return idx

    i1 = first_argmax(c)
    i2 = first_argmax([jnp.where(i1 == j, -jnp.inf, c[j]) for j in range(epg)])
    g1 = d[0]
    g2 = d[0]
    for j in range(1, epg):
        g1 = jnp.where(i1 == j, d[j], g1)
        g2 = jnp.where(i2 == j, d[j], g2)
    tot = g1 + g2
    return grp * epg + i1, grp * epg + i2, g1 / tot, g2 / tot


def _route_kernel(lg_ref, bias_ref, tri_ref, io_ref, fo_ref, cnt_ref, run_ref):
    first = (pl.program_id(0) == 0) & (pl.program_id(1) == 0)

    @pl.when(first)
    def _():
        run_ref[...] = jnp.zeros_like(run_ref)

    lg = lg_ref[...]
    n_e, tr = lg.shape
    e0, e1, g0, g1 = _route_rows(lg, bias_ref[...])
    erow = lax.broadcasted_iota(jnp.int32, (n_e, tr), 0)
    oh0 = erow == e0
    oh1 = erow == e1
    both = (oh0 | oh1).astype(F32)
    before = jnp.dot(both.astype(BF16), tri_ref[...], preferred_element_type=F32) + run_ref[...]
    r0 = jnp.sum(jnp.where(oh0, before, 0.0), axis=0, keepdims=True)
    r1 = jnp.sum(jnp.where(oh1, before, 0.0), axis=0, keepdims=True)
    run_ref[...] = run_ref[...] + jnp.sum(both, axis=1, keepdims=True)
    io_ref[...] = jnp.concatenate([e0, e1, r0.astype(jnp.int32), r1.astype(jnp.int32),
                                   jnp.zeros((4, tr), jnp.int32)], axis=0)
    fo_ref[...] = jnp.concatenate([g0, g1, jnp.zeros((6, tr), F32)], axis=0)
    cnt_ref[...] = jnp.broadcast_to(run_ref[...], cnt_ref.shape)


def _route(lgt, router_bias, tr):
    b, n_e, l = lgt.shape
    tri = (jnp.arange(tr)[:, None] < jnp.arange(tr)[None, :]).astype(BF16)
    io, fo, cnt = pl.pallas_call(
        _route_kernel,
        out_shape=(jax.ShapeDtypeStruct((b, 8, l), jnp.int32),
                   jax.ShapeDtypeStruct((b, 8, l), F32),
                   jax.ShapeDtypeStruct((n_e, LANES), F32)),
        grid=(b, l // tr),
        in_specs=[pl.BlockSpec((None, n_e, tr), lambda bi, t: (bi, 0, t)),
                  pl.BlockSpec((n_e, 1), lambda bi, t: (0, 0)),
                  pl.BlockSpec((tr, tr), lambda bi, t: (0, 0))],
        out_specs=(pl.BlockSpec((None, 8, tr), lambda bi, t: (bi, 0, t)),
                   pl.BlockSpec((None, 8, tr), lambda bi, t: (bi, 0, t)),
                   pl.BlockSpec((n_e, LANES), lambda bi, t: (0, 0))),
        scratch_shapes=[pltpu.VMEM((n_e, 1), F32)],
        compiler_params=_cparams(("arbitrary", "arbitrary")),
        name="route",
    )(lgt, router_bias.astype(F32).reshape(n_e, 1), tri)
    idx = jnp.stack([io[:, 0], io[:, 1]], axis=-1).reshape(b * l, TOP_K)
    rank = jnp.stack([io[:, 2], io[:, 3]], axis=-1).reshape(b * l, TOP_K)
    gate = jnp.stack([fo[:, 0], fo[:, 1]], axis=-1).reshape(b * l, TOP_K)
    return idx, rank, gate, cnt[:, 0].astype(jnp.int32)


def _row_copy(src_ref, src_row, dst_ref, dst_row, sem_ref):
    return pltpu.make_async_copy(src_ref.at[pl.ds(src_row, 1)], dst_ref.at[pl.ds(dst_row, 1)], sem_ref.at[0])


def _dispatch_kernel(dest_ref, h_ref, xz_ref, xb_ref, sem_ref):
    del xz_ref
    td = dest_ref.shape[1] // TOP_K

    def issue(r8, c):
        for j in range(8):
            r = r8 * 8 + j
            for k in range(TOP_K):
                _row_copy(h_ref, r, xb_ref, dest_ref[0, TOP_K * r + k], sem_ref).start(priority=k % 2)
        return c

    lax.fori_loop(0, td // 8, issue, 0)

    def drain(r, c):
        _row_copy(h_ref, 0, xb_ref, 0, sem_ref).wait()
        return c

    lax.fori_loop(0, TOP_K * td, drain, 0, unroll=8)


def _dispatch(dest2, h2, cap):
    t, d = h2.shape
    n_steps = dest2.shape[0]
    anyspec = pl.BlockSpec(memory_space=pl.ANY)
    return pl.pallas_call(
        _dispatch_kernel,
        out_shape=jax.ShapeDtypeStruct((cap, d), h2.dtype),
        grid=(n_steps,),
        in_specs=[pl.BlockSpec((None, 1, dest2.shape[2]), lambda i: (i, 0, 0), memory_space=pltpu.SMEM),
                  pl.BlockSpec((dest2.shape[2] // TOP_K, d), lambda i: (i, 0)), anyspec],
        out_specs=anyspec,
        scratch_shapes=[pltpu.SemaphoreType.DMA((1,))],
        input_output_aliases={2: 0},
        compiler_params=_cparams(("arbitrary",)),
        name="moe_dispatch",
    )(dest2, h2, jnp.zeros((cap, d), h2.dtype))


def _combine_kernel(dest_ref, yb_ref, xn_ref, mod_ref, gate_ref, g_ref, o_ref, buf_ref, sem_ref, *, final):
    tc = xn_ref.shape[0]

    def issue(r8, c):
        for j in range(8):
            r = r8 * 8 + j
            for k in range(TOP_K):
                pltpu.make_async_copy(yb_ref.at[pl.ds(dest_ref[0, TOP_K * r + k], 1)], buf_ref.at[k, pl.ds(r, 1)],
                                      sem_ref.at[0]).start(priority=k % 2)
        return c

    lax.fori_loop(0, tc // 8, issue, 0)

    def drain(r, c):
        pltpu.make_async_copy(yb_ref.at[pl.ds(0, 1)], buf_ref.at[0, pl.ds(0, 1)], sem_ref.at[0]).wait()
        return c

    lax.fori_loop(0, TOP_K * tc, drain, 0, unroll=8)
    gate = gate_ref[...].astype(BF16).astype(F32)
    y = buf_ref[0].astype(BF16).astype(F32) * gate[:, 0:1]
    for k in range(1, TOP_K):
        y = y + buf_ref[k].astype(BF16).astype(F32) * gate[:, k:k + 1]
    x = xn_ref[...] + mod_ref[5:6, :] * y
    o_ref[...] = _rms(x, g_ref[...]) if final else x


def _combine(dest2, yb, xn, mod, gate, g_final, tc, final):
    b, l, d = xn.shape
    nt = l // tc
    anyspec = pl.BlockSpec(memory_space=pl.ANY)
    return pl.pallas_call(
        functools.partial(_combine_kernel, final=final),
        out_shape=jax.ShapeDtypeStruct((b, l, d), F32),
        grid=(b, nt),
        in_specs=[pl.BlockSpec((None, 1, TOP_K * tc), lambda bi, t: (bi * nt + t, 0, 0), memory_space=pltpu.SMEM),
                  anyspec,
                  pl.BlockSpec((None, tc, d), lambda bi, t: (bi, t, 0)),
                  pl.BlockSpec((None, 6, d), lambda bi, t: (bi, 0, 0)),
                  pl.BlockSpec((None, tc, TOP_K), lambda bi, t: (bi, t, 0)),
                  pl.BlockSpec((1, d), lambda bi, t: (0, 0))],
        out_specs=pl.BlockSpec((None, tc, d), lambda bi, t: (bi, t, 0)),
        scratch_shapes=[pltpu.VMEM((TOP_K, tc, d), F32), pltpu.SemaphoreType.DMA((1,))],
        compiler_params=_cparams(("arbitrary", "arbitrary")),
        name="moe_combine",
    )(dest2, yb, xn, mod, gate.reshape(b, l, TOP_K), g_final)


def _moe(h2, lgt, router_bias, w1_b, w3_b, w2_b, tr):
    t, d = h2.shape
    n_experts = w1_b.shape[0]
    idx, rank, gate, counts = _route(lgt, router_bias, tr)
    s = t * TOP_K
    padded = (counts + MOE_ROWS - 1) // MOE_ROWS * MOE_ROWS
    pad_end = jnp.cumsum(padded)
    pad_start = pad_end - padded
    onehot = idx[..., None] == jnp.arange(n_experts)
    dest = rank + jnp.sum(jnp.where(onehot, pad_start, 0), axis=-1)
    n_blocks = -(-s // MOE_ROWS) + n_experts
    cap = n_blocks * MOE_ROWS
    dest2 = dest.reshape(t // MOE_TOK, 1, MOE_TOK * TOP_K).astype(jnp.int32)
    xb = _dispatch(dest2, h2, cap)
    block_start = jnp.arange(n_blocks, dtype=pad_end.dtype) * MOE_ROWS
    block_e = jnp.minimum(jnp.sum(pad_end[None, :] <= block_start[:, None], axis=1), n_experts - 1).astype(jnp.int32)
    n_active = (pad_end[-1] // MOE_ROWS).astype(jnp.int32).reshape(1)
    yb = _moe_blocks(block_e, n_active, xb, w1_b, w3_b, w2_b)
    return yb, dest2, gate


def _final_kernel(x_ref, g_ref, o_ref):
    o_ref[...] = _rms(x_ref[...], g_ref[...])


def _final_norm(x2d, g):
    t, d = x2d.shape
    tm = min(t, 512)
    return pl.pallas_call(
        _final_kernel,
        out_shape=jax.ShapeDtypeStruct((t, d), F32),
        grid=(t // tm,),
        in_specs=[pl.BlockSpec((tm, d), lambda i: (i, 0)), pl.BlockSpec((1, d), lambda i: (0, 0))],
        out_specs=pl.BlockSpec((tm, d), lambda i: (i, 0)),
        compiler_params=_cparams(("arbitrary",)),
        name="final_norm",
    )(x2d, g)


def _alibi_slopes(n_heads):
    return 2.0 ** (-8.0 * (jnp.arange(n_heads, dtype=F32) + 1.0) / n_heads)


def _spre_kernel(x_ref, sh_ref, sc_ref, g_ref, w_ref, o_ref):
    h = _rms(x_ref[...], g_ref[...]) * (1.0 + sc_ref[...]) + sh_ref[...]
    o_ref[...] = jnp.dot(h.astype(BF16), w_ref[...], preferred_element_type=F32)


def _sample_pre(x, sh, sc, g, w_in_b):
    db, d = x.shape
    n = w_in_b.shape[1]
    full = lambda shape: pl.BlockSpec(shape, lambda i: (0,) * len(shape))
    return pl.pallas_call(
        _spre_kernel,
        out_shape=jax.ShapeDtypeStruct((db, n), F32),
        grid=(1,),
        in_specs=[full((db, d)), full((db, d)), full((db, d)), full((1, d)), full((d, n))],
        out_specs=full((db, n)),
        compiler_params=_cparams(("arbitrary",)),
        name="sample_pre",
    )(x, sh, sc, g, w_in_b)


def _sgate_kernel(q_ref, km_ref, o_ref):
    a, n_fp = km_ref.shape
    n_heads = a // HEAD_DIM
    hrow = lax.broadcasted_iota(jnp.int32, (n_heads, a), 0)
    lane = lax.broadcasted_iota(jnp.int32, (n_heads, a), 1)
    qrows = jnp.where(lane // HEAD_DIM == hrow, jnp.broadcast_to(q_ref[...], (n_heads, a)), 0.0).astype(BF16)
    gate = jnp.dot(qrows, km_ref[...].astype(BF16), preferred_element_type=F32)
    jl = lax.broadcasted_iota(jnp.int32, gate.shape, 1)
    ol = lax.broadcasted_iota(jnp.int32, o_ref.shape, 1)
    out = jnp.zeros(o_ref.shape, jnp.int32)
    for r in range(MOBA_TOPK):
        mx = jnp.max(gate, axis=1, keepdims=True)
        idx = jnp.min(jnp.where(gate == mx, jl, n_fp), axis=1, keepdims=True)
        out = jnp.where(ol == r, idx, out)
        gate = jnp.where(jl == idx, -jnp.inf, gate)
    o_ref[...] = out


def _sample_gate(q3, kmean_l):
    db, _, a = q3.shape
    n_fp = kmean_l.shape[2]
    n_heads = a // HEAD_DIM
    return pl.pallas_call(
        _sgate_kernel,
        out_shape=jax.ShapeDtypeStruct((db, n_heads, LANES), jnp.int32),
        grid=(db,),
        in_specs=[pl.BlockSpec((None, 1, a), lambda b: (b, 0, 0)),
                  pl.BlockSpec((None, a, n_fp), lambda b: (b, 0, 0))],
        out_specs=pl.BlockSpec((None, n_heads, LANES), lambda b: (b, 0, 0)),
        compiler_params=_cparams(("arbitrary",)),
        name="sample_gate",
    )(q3, kmean_l)


def _sattn_copies(pt_ref, sel_ref, ck_ref, cv_ref, kb_ref, vb_ref, sem_ref, layer, slot, b, n_heads, ppb):
    out = []
    for h in range(n_heads):
        for r in range(MOBA_TOPK):
            blk = sel_ref[b, h * MOBA_TOPK + r]
            for pg in range(ppb):
                phys = pt_ref[b, blk * ppb + pg]
                i = r * ppb + pg
                out.append(pltpu.make_async_copy(ck_ref.at[layer, phys, h], kb_ref.at[slot, h, i], sem_ref.at[slot]))
                out.append(pltpu.make_async_copy(cv_ref.at[layer, phys, h], vb_ref.at[slot, h, i], sem_ref.at[slot]))
    return out


def _sattn_kernel(pt_ref, sel_ref, q_ref, k_ref, v_ref, ck_ref, cv_ref, o_ref, kb_ref, vb_ref, sem_ref, *, layer, past_len):
    b = pl.program_id(0)
    n_b = pl.num_programs(0)
    slot = b % 2
    n_heads, n_pg, hd, page = kb_ref.shape[1:]
    ppb = MOBA_BLOCK // page
    args = (pt_ref, sel_ref, ck_ref, cv_ref, kb_ref, vb_ref, sem_ref, layer)

    @pl.when(b == 0)
    def _():
        for c in _sattn_copies(*args, 0, b, n_heads, ppb):
            c.start()

    @pl.when(b + 1 < n_b)
    def _():
        for c in _sattn_copies(*args, 1 - slot, b + 1, n_heads, ppb):
            c.start()

    for c in _sattn_copies(*args, slot, b, n_heads, ppb):
        c.wait()

    n_sel = n_pg * page
    kpos = lax.broadcasted_iota(jnp.int32, (1, n_sel), 1)
    outs = []
    for h in range(n_heads):
        hs = slice(h * hd, (h + 1) * hd)
        slope = 2.0 ** (-8.0 * (h + 1.0) / n_heads)
        qf = (q_ref[:, hs] * HEAD_DIM ** -0.5).astype(BF16)
        q8 = jnp.broadcast_to(qf, (8, hd))
        l_sel = jnp.concatenate([jnp.dot(q8, kb_ref[slot, h, i].astype(BF16), preferred_element_type=F32)
                                 for i in range(n_pg)], axis=1)[0:1]
        blk_of = jnp.zeros((1, n_sel), jnp.int32)
        for r in range(MOBA_TOPK):
            blk_of = jnp.where(kpos // MOBA_BLOCK == r, sel_ref[b, h * MOBA_TOPK + r], blk_of)
        pos_sel = blk_of * MOBA_BLOCK + kpos % MOBA_BLOCK
        l_sel = l_sel - slope * (past_len - pos_sel).astype(F32)
        kown = k_ref[:, hs].astype(BF16).astype(F32)
        vown = v_ref[:, hs].astype(BF16).astype(F32)
        l_own = jnp.sum(qf.astype(F32) * kown, axis=1, keepdims=True)
        mx = jnp.maximum(jnp.max(l_sel, axis=1, keepdims=True), l_own)
        e_sel = jnp.exp(l_sel - mx)
        e_own = jnp.exp(l_own - mx)
        den = jnp.sum(e_sel, axis=1, keepdims=True) + e_own
        p8 = jnp.broadcast_to((e_sel / den).astype(BF16), (8, n_sel))
        p_own = (e_own / den).astype(BF16).astype(F32)
        o_h = p_own * vown
        for i in range(n_pg):
            o_h = o_h + lax.dot_general(p8[:, i * page:(i + 1) * page], vb_ref[slot, h, i].astype(BF16), _NT,
                                        preferred_element_type=F32)[0:1]
        outs.append(o_h)
    o_ref[...] = jnp.concatenate(outs, axis=1)


def _sample_attn(page_table, sel, q3, k3, v3, cache_kt, cache_vt, layer, past_len):
    db, _, a = q3.shape
    n_heads, hd, page = cache_kt.shape[2:]
    ppb = MOBA_BLOCK // page
    row = pl.BlockSpec((None, 1, a), lambda b, pt, sl: (b, 0, 0))
    anyspec = pl.BlockSpec(memory_space=pl.ANY)
    grid_spec = pltpu.PrefetchScalarGridSpec(
        num_scalar_prefetch=2,
        grid=(db,),
        in_specs=[row, row, row, anyspec, anyspec],
        out_specs=row,
        scratch_shapes=[pltpu.VMEM((2, n_heads, MOBA_TOPK * ppb, hd, page), F32),
                        pltpu.VMEM((2, n_heads, MOBA_TOPK * ppb, hd, page), F32),
                        pltpu.SemaphoreType.DMA((2,))])
    return pl.pallas_call(
        functools.partial(_sattn_kernel, layer=layer, past_len=past_len),
        out_shape=jax.ShapeDtypeStruct((db, 1, a), F32),
        grid_spec=grid_spec,
        compiler_params=_cparams(("arbitrary",)),
        name="sample_attn",
    )(page_table, sel, q3, k3, v3, cache_kt, cache_vt)


def _spost_kernel(x_ref, attn_ref, u_ref, st_ref, ga_ref, sh_ref, sc_ref, wp_ref, ps_ref, wo_ref, g_ref, wr_ref,
                  xo_ref, h2_ref, lg_ref, *, past_len):
    a = attn_ref.shape[-1]
    u = u_ref[...]
    st = st_ref[...]
    n_st = st.shape[1]
    gw = a // len(POOL_WINDOWS)
    srow = lax.broadcasted_iota(jnp.int32, (1, n_st, a), 1)
    lane = lax.broadcasted_iota(jnp.int32, (1, n_st, a), 2)
    win = jnp.zeros((1, n_st, a), jnp.int32)
    cnt = jnp.zeros((1, a), F32)
    lane2 = lax.broadcasted_iota(jnp.int32, (1, a), 1)
    for g, w in enumerate(POOL_WINDOWS):
        win = jnp.where(lane // gw == g, w, win)
        cnt = jnp.where(lane2 // gw == g, float(min(w, past_len + 1)), cnt)
    wsum = jnp.sum(jnp.where(srow >= n_st + 1 - win, st, 0.0), axis=1) + u
    diff = (wsum / cnt - u).astype(BF16)
    pools = [jnp.dot(diff[:, g * gw:(g + 1) * gw], wp_ref[g], preferred_element_type=F32)
             for g in range(len(POOL_WINDOWS))]
    pool = jnp.concatenate(pools, axis=1) * ps_ref[...]
    mo = (jnp.dot(attn_ref[...].astype(BF16), wo_ref[0:a, :], preferred_element_type=F32)
          + jnp.dot(pool.astype(BF16), wo_ref[a:, :], preferred_element_type=F32))
    xn = x_ref[...] + ga_ref[...] * mo
    xo_ref[...] = xn
    hb = (_rms(xn, g_ref[...]) * (1.0 + sc_ref[...]) + sh_ref[...]).astype(BF16)
    h2_ref[...] = hb
    lg_ref[...] = lax.dot_general(wr_ref[...], hb, _NT, preferred_element_type=F32)


def _sample_post(x, attn, u, state, ga1, sh2, sc2, w_pool_b, pool_scale, w_out_b, g_ffn, wr_b, past_len):
    db, d = x.shape
    ne = wr_b.shape[0]
    ins = (x, attn, u, state, ga1, sh2, sc2, w_pool_b, pool_scale, w_out_b, g_ffn, wr_b)
    full = lambda arr: pl.BlockSpec(arr.shape, lambda i, n=arr.ndim: (0,) * n)
    outs = (jax.ShapeDtypeStruct((db, d), F32), jax.ShapeDtypeStruct((db, d), BF16), jax.ShapeDtypeStruct((ne, db), F32))
    return pl.pallas_call(
        functools.partial(_spost_kernel, past_len=past_len),
        out_shape=outs,
        grid=(1,),
        in_specs=[full(x_) for x_ in ins],
        out_specs=tuple(full(o) for o in outs),
        compiler_params=_cparams(("arbitrary",)),
        name="sample_post",
    )(*ins)


def _smoe_kernel(h_ref, cb_ref, w1_ref, w3_ref, w2_ref, xn_ref, ga_ref, o_ref, acc_ref):
    e = pl.program_id(0)

    @pl.when(e == 0)
    def _():
        acc_ref[...] = jnp.zeros_like(acc_ref)

    h = h_ref[...]
    a = jnp.dot(h, w1_ref[...], preferred_element_type=F32)
    g = jnp.dot(h, w3_ref[...], preferred_element_type=F32)
    y = jnp.dot(((a * jax.nn.sigmoid(a)) * g).astype(BF16), w2_ref[...], preferred_element_type=F32)
    acc_ref[...] += y.astype(BF16).astype(F32) * cb_ref[...].astype(BF16).astype(F32)

    @pl.when(e == pl.num_programs(0) - 1)
    def _():
        o_ref[...] = xn_ref[...] + ga_ref[...] * acc_ref[...]


def _sample_moe(h2b, comb, w1_b, w3_b, w2_b, xn, ga2):
    db, d = h2b.shape
    n_e, _, de = w1_b.shape
    cst = lambda e: (0, 0)
    return pl.pallas_call(
        _smoe_kernel,
        out_shape=jax.ShapeDtypeStruct((db, d), F32),
        grid=(n_e,),
        in_specs=[pl.BlockSpec((db, d), cst),
                  pl.BlockSpec((None, db, 1), lambda e: (e, 0, 0)),
                  pl.BlockSpec((None, d, de), lambda e: (e, 0, 0)),
                  pl.BlockSpec((None, d, de), lambda e: (e, 0, 0)),
                  pl.BlockSpec((None, de, d), lambda e: (e, 0, 0)),
                  pl.BlockSpec((db, d), cst),
                  pl.BlockSpec((db, d), cst)],
        out_specs=pl.BlockSpec((db, d), cst),
        scratch_shapes=[pltpu.VMEM((db, d), F32)],
        compiler_params=_cparams(("arbitrary",)),
        name="sample_moe",
    )(h2b, comb, w1_b, w3_b, w2_b, xn, ga2)


def _sample_layer(x, mod_s, layer, kmean_l, cache_kt, cache_vt, page_table, state_l, g_mix_l, w_in_b_l, w_pool_b_l,
                  pool_scale_l, w_out_b_l, g_ffn_l, wr_b, router_bias, w1_b_l, w3_b_l, w2_b_l, past_len):
    db, d = x.shape
    a = kmean_l.shape[1]
    n_e = w1_b_l.shape[0]
    sh1, sc1, ga1, sh2, sc2, ga2 = [mod_s[:, i] for i in range(6)]
    r = _sample_pre(x, sh1, sc1, g_mix_l, w_in_b_l)
    q3, k3, v3, u = r[:, None, :a], r[:, None, a:2 * a], r[:, None, 2 * a:3 * a], r[:, 3 * a:]
    sel = _sample_gate(q3, kmean_l)[:, :, :MOBA_TOPK].reshape(db, -1)
    attn = _sample_attn(page_table, sel, q3, k3, v3, cache_kt, cache_vt, layer, past_len)[:, 0]
    xn, h2b, lgt = _sample_post(x, attn, u, state_l, ga1, sh2, sc2, w_pool_b_l, pool_scale_l, w_out_b_l,
                                g_ffn_l, wr_b, past_len)
    idx, _, gate, _ = _route(lgt[None], router_bias, db)
    comb = jnp.sum(jnp.where(idx[None, :, :] == jnp.arange(n_e)[:, None, None], gate[None], 0.0), axis=-1)
    x_out = _sample_moe(h2b, comb[..., None], w1_b_l, w3_b_l, w2_b_l, xn, ga2)
    return x_out, k3, v3, u


def kernel(x_prompt, x_sample, cache_k, cache_v, state_pool, page_table, c_prompt, c_sample,
           w_ada, b_ada, g_mix, w_in, w_pool, pool_scale, w_out, g_ffn, w_router, router_bias,
           w1, w3, w2, g_final):
    b, l, d = x_prompt.shape
    db, ds, _ = x_sample.shape
    depth = w_ada.shape[0]
    n_heads = cache_k.shape[3]
    a = n_heads * HEAD_DIM
    n_experts = w1.shape[1]
    tm = 512
    assert l % tm == 0 and tm % MOBA_BLOCK == 0 and a % LANES == 0 and n_experts <= 16

    mod = _ada(jnp.concatenate([c_prompt, c_sample], axis=0), w_ada, b_ada)
    mod = mod.reshape(depth, b + db, 6, d)
    slopes = _alibi_slopes(n_heads)

    w_q, w_k, w_v = w_in[:, :, :a], w_in[:, :, a:2 * a], w_in[:, :, 2 * a:3 * a]

    def head_pad(w):
        w = w.reshape(depth, d, n_heads, HEAD_DIM)
        return jnp.pad(w, ((0, 0), (0, 0), (0, 0), (0, HEAD_PAD - HEAD_DIM))).reshape(depth, d, n_heads * HEAD_PAD)

    wn_b = w_in[:, :, a:].astype(BF16)
    wa_b = jnp.concatenate([head_pad(w_q), head_pad(w_k)], axis=-1).astype(BF16)
    wvt_b = jnp.swapaxes(w_v, 1, 2).astype(BF16)
    slopes_np = 2.0 ** (-8.0 * (np.arange(n_heads, dtype=np.float64) + 1.0) / n_heads)
    key_term = (slopes_np[:, None] * np.arange(MOBA_BLOCK)[None, :]).astype(np.float32)
    assert np.array_equal(key_term.astype(BF16).astype(np.float32), key_term), \
        "ALiBi key term must be exact in bfloat16 to ride in the padded key lanes"
    alibi_lane = np.zeros((n_heads, HEAD_PAD), np.float32)
    alibi_lane[:, HEAD_DIM] = 1.0
    orow = jnp.asarray(alibi_lane.reshape(1, -1))
    srow = jnp.asarray((alibi_lane * slopes_np[:, None].astype(np.float32)).reshape(1, -1))
    w_out_b = w_out.astype(BF16)
    w_pool_b = w_pool.astype(BF16)
    w1_b, w3_b, w2_b = w1.astype(BF16), w3.astype(BF16), w2.astype(BF16)
    wr_b = w_router.T.astype(BF16)

    w_in_b = w_in.astype(BF16)
    page = cache_k.shape[2]
    n_pool = cache_k.shape[1]
    past_len = page_table.shape[1] * page
    n_fp = past_len // MOBA_BLOCK
    assert ds == 1 and past_len % MOBA_BLOCK == 0 and n_fp >= MOBA_TOPK and MOBA_BLOCK % page == 0
    cache_kt = jnp.transpose(cache_k, (0, 1, 3, 4, 2))
    cache_vt = jnp.transpose(cache_v, (0, 1, 3, 4, 2))
    cache_kt4 = cache_kt.reshape(depth, n_pool, a, page)

    xp, xs = x_prompt, x_sample.reshape(db, d)
    kp, vp, up, kq, vq, uq = [], [], [], [], [], []
    for li in range(depth):
        mod_p = mod[li, :b]
        k, v, u, qa, ka, vt, kmean = _qkvu(xp, mod_p, g_mix[li][None], wn_b[li], wa_b[li], wvt_b[li], srow, orow, tm)
        kp.append(k.reshape(b, l, n_heads, HEAD_DIM))
        vp.append(v.reshape(b, l, n_heads, HEAD_DIM))
        attn, kmean_s = _moba_prompt(slopes, page_table, qa, ka, vt, kmean, cache_kt4, li, n_fp)
        xn, h2, lg = _mix(xp, attn, u, mod_p, w_pool_b[li], pool_scale[li][None], w_out_b[li],
                          g_ffn[li][None], wr_b, tm)
        yb, dest2, gate = _moe(h2.reshape(b * l, d), lg, router_bias, w1_b[li], w3_b[li], w2_b[li], tm)
        xp = _combine(dest2, yb, xn, mod_p, gate, g_final[None], MOE_TOK, li == depth - 1)
        up.append(u[:, l - POOL_STATE_LEN:])

        xs, k3, v3, u_s = _sample_layer(
            xs, mod[li, b:], li, kmean_s, cache_kt, cache_vt, page_table, state_pool[li], g_mix[li][None],
            w_in_b[li], w_pool_b[li], pool_scale[li][None], w_out_b[li], g_ffn[li][None], wr_b, router_bias,
            w1_b[li], w3_b[li], w2_b[li], past_len)
        kq.append(k3.reshape(db, ds, n_heads, HEAD_DIM))
        vq.append(v3.reshape(db, ds, n_heads, HEAD_DIM))
        uq.append(jnp.concatenate([state_pool[li][:, ds:], u_s[:, None]], axis=1))

    y_prompt = xp
    y_sample = _final_norm(xs, g_final[None]).reshape(db, ds, d)
    return (y_prompt, y_sample, jnp.stack(kp), jnp.stack(vp), jnp.stack(up),
            jnp.stack(kq), jnp.stack(vq), jnp.stack(uq))
```
